```python
import math
import jax, jax.numpy as jnp
from jax import lax
import numpy as np

D_MODEL = 1024
BATCH = 8
SEQ = 2048
DEPTH = 4

GRID_W = 64
MIX_WIDTH = 2 * D_MODEL
HG_WIDTH = MIX_WIDTH // 2
HG_HEAD_DIM = 128
HG_HEADS = HG_WIDTH // HG_HEAD_DIM
HG_CHUNK = 32
NA_WIDTH = MIX_WIDTH - HG_WIDTH
NA_HEAD_DIM = 64
NA_HEADS = NA_WIDTH // NA_HEAD_DIM
NA_ROWS_MAX = 8
NA_KC = 16
NA_SPAN = 2 * NA_KC
IN_COLS = 5 * HG_WIDTH + 4 * NA_WIDTH
ADA_COLS = 3 * D_MODEL
DEEPNORM_ALPHA = (2 * DEPTH) ** 0.25
DEEPNORM_BETA = (8 * DEPTH) ** -0.25
LN_EPS = 1e-5
RMS_EPS = 1e-6

kernel_name = "hymba_hgrn2_natten_deepnorm_encoder"


def _layernorm(x):
    xf = x.astype(jnp.float32)
    mu = jnp.mean(xf, axis=-1, keepdims=True)
    var = jnp.mean(jnp.square(xf - mu), axis=-1, keepdims=True)
    return (xf - mu) * lax.rsqrt(var + LN_EPS)


def _rmsnorm(x, g):
    xf = x.astype(jnp.float32)
    return xf * lax.rsqrt(jnp.mean(jnp.square(xf), axis=-1, keepdims=True) + RMS_EPS) * g.astype(jnp.float32)


def _hgrn2_lower_bounds(lb_logits):
    p = jax.nn.softmax(lb_logits.astype(jnp.float32), axis=0)
    cs = jnp.cumsum(p, axis=0)
    return cs - cs[0:1]


def _hgrn2_chunk_scan(q, k, v, log_f):
    B, H, S, DK = q.shape
    DV = v.shape[-1]
    C = HG_CHUNK
    N = S // C
    q, k, log_f = (t.reshape(B, H, N, C, DK) for t in (q, k, log_f))
    v = v.reshape(B, H, N, C, DV)
    b = jnp.cumsum(log_f, axis=3)
    b_mid = b[:, :, :, C // 2 - 1:C // 2]
    b_last = b[:, :, :, C - 1:]
    scores = jnp.einsum('bhntd,bhnsd->bhnts', q * jnp.exp(b - b_mid), k * jnp.exp(b_mid - b))
    lower_tri = jnp.tril(jnp.ones((C, C), dtype=bool))
    scores = jnp.where(lower_tri, scores, 0.0)
    o_intra = jnp.einsum('bhnts,bhnsv->bhntv', scores, v)
    q_inter = q * jnp.exp(b)
    k_end = k * jnp.exp(b_last - b)
    decay = jnp.exp(b_last[:, :, :, 0])

    def step(state, xs):
        qc, kc, vc, dc = xs
        o = jnp.einsum('bhtd,bhdv->bhtv', qc, state)
        state = dc[..., None] * state + jnp.einsum('bhtd,bhtv->bhdv', kc, vc)
        return state, o

    xs = tuple(jnp.moveaxis(t, 2, 0) for t in (q_inter, k_end, v, decay))
    _, o_inter = lax.scan(step, jnp.zeros((B, H, DK, DV), q.dtype), xs)
    o = o_intra + jnp.moveaxis(o_inter, 0, 2)
    return o.reshape(B, H, S, DV)


def _hgrn2_branch(q_in, i_in, zf_fw, zf_bw, g_in, lb_fw, lb_bw, norm_g):
    B, S, _ = q_in.shape

    def heads(t):
        return t.astype(jnp.float32).reshape(B, S, HG_HEADS, HG_HEAD_DIM).transpose(0, 2, 1, 3)

    q = heads(q_in)
    v = heads(i_in)

    def direction(z, lb, flip):
        f = heads(lb + (1.0 - lb) * jax.nn.sigmoid(z.astype(jnp.float32)))
        qq, vv = q, v
        if flip:
            f, qq, vv = (jnp.flip(t, axis=2) for t in (f, q, v))
        o = _hgrn2_chunk_scan(qq, 1.0 - f, vv, jnp.log(f))
        return jnp.flip(o, axis=2) if flip else o

    o = direction(zf_fw, lb_fw, False) + direction(zf_bw, lb_bw, True)
    o = _rmsnorm(o.transpose(0, 2, 1, 3), norm_g).reshape(B, S, HG_WIDTH)
    return o.astype(g_in.dtype) * jax.nn.silu(g_in)


def _neighbourhood_attention(q, k, v, rpb):
    B, H, R, W, dh = q.shape
    KR = min(NA_ROWS_MAX, R)
    nqb = W // NA_KC
    qc = jnp.arange(W)
    cs = jnp.clip(qc - NA_KC // 2, 0, W - NA_KC)
    ks = jnp.minimum(cs[::NA_KC], W - NA_SPAN)
    key_cols = ks[:, None] + jnp.arange(NA_SPAN)
    qcols = qc.reshape(nqb, NA_KC)
    cs_b = cs.reshape(nqb, NA_KC)
    kc3 = key_cols[:, None, :]
    col_mask = (kc3 >= cs_b[:, :, None]) & (kc3 < cs_b[:, :, None] + NA_KC)
    col_off = jnp.clip(kc3 - qcols[:, :, None] + NA_KC - 1, 0, 2 * NA_KC - 2)
    col_bias = rpb.astype(jnp.float32)[:, :, col_off]
    scale = NA_HEAD_DIM ** -0.5

    def one_row(r):
        rs = jnp.clip(r - KR // 2, 0, R - KR)
        k_rows = lax.dynamic_slice_in_dim(k, rs, KR, axis=2)
        v_rows = lax.dynamic_slice_in_dim(v, rs, KR, axis=2)
        k_blk = jnp.take(k_rows, key_cols, axis=3)
        v_blk = jnp.take(v_rows, key_cols, axis=3)
        q_row = lax.dynamic_index_in_dim(q, r, axis=2, keepdims=False).reshape(B, H, nqb, NA_KC, dh)
        s = jnp.einsum('bhjqd,bhrjcd->bhjqrc', q_row, k_blk).astype(jnp.float32) * scale
        row_off = rs + jnp.arange(KR) - r + NA_ROWS_MAX - 1
        bias = jnp.take(col_bias, row_off, axis=1).transpose(0, 2, 3, 1, 4)
        s = jnp.where(col_mask[:, :, None, :], s + bias[None], -jnp.inf)
        p = jax.nn.softmax(s.reshape(B, H, nqb, NA_KC, KR * NA_SPAN), axis=-1)
        p = p.reshape(B, H, nqb, NA_KC, KR, NA_SPAN).astype(v.dtype)
        o = jnp.einsum('bhjqrc,bhrjcd->bhjqd', p, v_blk)
        return o.reshape(B, H, W, dh)

    return lax.map(one_row, jnp.arange(R))


def setup_inputs(seed: int = 0) -> dict:
    key = jax.random.key(seed)
    ks = jax.random.split(key, 12)
    f32 = jnp.float32
    x = jax.random.normal(ks[0], (BATCH, SEQ, D_MODEL), f32)
    c = jax.random.normal(ks[1], (BATCH, D_MODEL), f32)
    ada_w = jax.random.normal(ks[2], (DEPTH, D_MODEL, ADA_COLS), f32) * D_MODEL ** -0.5
    ada_b = jax.random.normal(ks[3], (DEPTH, ADA_COLS), f32) * 0.02
    w_in = jax.random.normal(ks[4], (DEPTH, D_MODEL, IN_COLS), f32) * D_MODEL ** -0.5
    lb_logits = jax.random.normal(ks[5], (2, DEPTH, HG_WIDTH), f32)
    hg_norm_g = 1.0 + 0.02 * jax.random.normal(ks[6], (DEPTH, HG_HEAD_DIM), f32)
    rpb = jax.random.normal(ks[7], (DEPTH, NA_HEADS, 2 * NA_ROWS_MAX - 1, 2 * NA_KC - 1), f32) * 0.1
    w_out = jax.random.normal(ks[8], (DEPTH, MIX_WIDTH, D_MODEL), f32) * (MIX_WIDTH ** -0.5 * DEEPNORM_BETA)
    ln_g = 1.0 + 0.02 * jax.random.normal(ks[9], (DEPTH, D_MODEL), f32)
    ln_b = 0.02 * jax.random.normal(ks[10], (DEPTH, D_MODEL), f32)
    return {"x": x, "c": c, "ada_w": ada_w, "ada_b": ada_b, "w_in": w_in, "lb_logits": lb_logits,
            "hg_norm_g": hg_norm_g, "rpb": rpb, "w_out": w_out, "ln_g": ln_g, "ln_b": ln_b}


def reference(x, c, ada_w, ada_b, w_in, lb_logits, hg_norm_g, rpb, w_out, ln_g, ln_b):
    B, S, D = x.shape
    R = S // GRID_W
    lb_fw_all = _hgrn2_lower_bounds(lb_logits[0])
    lb_bw_all = _hgrn2_lower_bounds(lb_logits[1])
    c_act = jax.nn.silu(c)
    split_pts = np.cumsum([HG_WIDTH] * 5 + [NA_WIDTH] * 3).tolist()

    def na_heads(t):
        return t.reshape(B, R, GRID_W, NA_HEADS, NA_HEAD_DIM).transpose(0, 3, 1, 2, 4)

    for l in range(DEPTH):
        mod = c_act @ ada_w[l] + ada_b[l]
        shift, scale, gate = jnp.split(mod, 3, axis=-1)
        h = (_layernorm(x) * (1.0 + scale[:, None, :]) + shift[:, None, :]).astype(x.dtype)
        proj = h @ w_in[l]
        q_h, i_h, zf_fw, zf_bw, g_h, q_n, k_n, v_n, g_n = jnp.split(proj, split_pts, axis=-1)
        y_hg = _hgrn2_branch(q_h, i_h, zf_fw, zf_bw, g_h, lb_fw_all[l], lb_bw_all[l], hg_norm_g[l])
        o_na = _neighbourhood_attention(na_heads(q_n), na_heads(k_n), na_heads(v_n), rpb[l])
        o_na = o_na.transpose(1, 0, 3, 2, 4).reshape(B, S, NA_WIDTH)
        y_na = o_na * jax.nn.silu(g_n)
        y = jnp.concatenate([y_hg, y_na], axis=-1) @ w_out[l]
        z = DEEPNORM_ALPHA * x + gate[:, None, :] * y
        x = (_layernorm(z) * ln_g[l] + ln_b[l]).astype(x.dtype)
    return x
```

```python
import functools

import numpy as np
import jax
import jax.numpy as jnp
from jax import lax
from jax.experimental import pallas as pl
from jax.experimental.pallas import tpu as pltpu

GRID_W = 64
HG_HEAD_DIM = 128
HG_CHUNK = 32
NA_HEAD_DIM = 64
NA_ROWS_MAX = 8
NA_KC = 16
LN_EPS = 1e-5
RMS_EPS = 1e-6
MASK_VALUE = -1e30

LANES = 128
MXU_DIM = 256
VMEM_LIMIT_BYTES = 56 * 1024 * 1024

F32 = jnp.float32
BF16 = jnp.bfloat16
NT_DIMS = (((1,), (1,)), ((), ()))


def _params(n_axes):
    return pltpu.CompilerParams(dimension_semantics=("arbitrary",) * n_axes,
                                vmem_limit_bytes=VMEM_LIMIT_BYTES)


def _ada_kernel(c_ref, w_ref, b_ref, o_ref):
    c = c_ref[...]
    a = c * jax.nn.sigmoid(c)
    o_ref[0] = jnp.dot(a, w_ref[0], precision=lax.Precision.HIGHEST,
                       preferred_element_type=F32) + b_ref[0]


def _ada_modulation(c, ada_w, ada_b):
    depth, d, cols = ada_w.shape
    bsz = c.shape[0]
    nj = cols // d
    return pl.pallas_call(
        _ada_kernel,
        grid=(depth, nj),
        in_specs=[pl.BlockSpec((bsz, d), lambda l, j: (0, 0)),
                  pl.BlockSpec((1, d, d), lambda l, j: (l, 0, j)),
                  pl.BlockSpec((1, 1, d), lambda l, j: (l, 0, j))],
        out_specs=pl.BlockSpec((1, bsz, d), lambda l, j: (l, 0, j)),
        out_shape=jax.ShapeDtypeStruct((depth, bsz, cols), F32),
        compiler_params=_params(2),
        name="ada_modulation",
    )(c, ada_w, ada_b.reshape(depth, 1, cols))


def _inproj_kernel(x_ref, mod_ref, w_ref, o_ref, h_ref, *, d):
    @pl.when(pl.program_id(1) == 0)
    def _():
        x = x_ref[...]
        mu = jnp.mean(x, axis=-1, keepdims=True)
        xc = x - mu
        var = jnp.mean(xc * xc, axis=-1, keepdims=True)
        hn = xc * lax.rsqrt(var + LN_EPS)
        m = mod_ref[0, 0]
        h_ref[...] = (hn * (1.0 + m[:, d:2 * d]) + m[:, 0:d]).astype(BF16)

    o_ref[...] = jnp.dot(h_ref[...], w_ref[0], preferred_element_type=F32).astype(o_ref.dtype)


def _in_projection(x2, mod4, w_in_bf, layer, seq, tm, tn):
    m, d = x2.shape
    n = w_in_bf.shape[2]
    return pl.pallas_call(
        functools.partial(_inproj_kernel, d=d),
        grid=(m // tm, n // tn),
        in_specs=[pl.BlockSpec((tm, d), lambda i, j: (i, 0)),
                  pl.BlockSpec((1, 1, 1, 3 * d), lambda i, j: (layer, (i * tm) // seq, 0, 0)),
                  pl.BlockSpec((1, d, tn), lambda i, j: (layer, 0, j))],
        out_specs=pl.BlockSpec((tm, tn), lambda i, j: (i, j)),
        out_shape=jax.ShapeDtypeStruct((m, n), BF16),
        scratch_shapes=[pltpu.VMEM((tm, d), BF16)],
        compiler_params=_params(2),
        name="ln_in_projection",
    )(x2, mod4, w_in_bf)


def _chunk_sum_matrices(tile, chunk):
    t = lax.broadcasted_iota(jnp.int32, (tile, tile), 0)
    u = lax.broadcasted_iota(jnp.int32, (tile, tile), 1)
    same = (t // chunk) == (u // chunk)
    start = (t // chunk) * chunk
    n_chunks = tile // chunk
    sel_rows = 16
    cj = lax.broadcasted_iota(jnp.int32, (sel_rows, tile), 0)
    cu = lax.broadcasted_iota(jnp.int32, (sel_rows, tile), 1)
    csel = jnp.where((cu // chunk == cj) & (cj < n_chunks), 1.0, 0.0)

    def stack(incl, incl_mid, rest):
        f = lambda m: jnp.where(same & m, 1.0, 0.0)
        return jnp.concatenate([f(incl) - f(incl_mid), f(incl), f(rest), csel], axis=0).astype(BF16)

    mid_f = start + (chunk // 2 - 1)
    fw = stack(u <= t, u <= mid_f, u > t)
    mid_b = start + chunk // 2
    bw = stack(u >= t, u >= mid_b, u < t)
    return fw, bw


def _hgrn_kernel(q_ref, i_ref, zf_ref, zb_ref, g_ref, lbl_ref, ng_ref, o_ref,
                 qa_s, ka_s, qi_s, ke_s, vt_s, st_s, dec_s, *, layer):
    seq = q_ref.shape[1]
    dk = q_ref.shape[2]
    tile = MXU_DIM
    chunk = HG_CHUNK
    cpt = tile // chunk
    n_tiles = seq // tile
    n_chunks = seq // chunk

    lg = lbl_ref[...]
    e = jnp.exp(lg - jnp.max(lg, axis=1, keepdims=True))
    p = e / jnp.sum(e, axis=1, keepdims=True)
    if layer == 0:
        lbs = [jnp.zeros((1, dk), F32)] * 2
    else:
        lbs = [jnp.sum(p[d, 1:layer + 1, :], axis=0, keepdims=True) for d in range(2)]

    m_fw, m_bw = _chunk_sum_matrices(tile, chunk)
    mats = (m_fw, m_bw)
    z_refs = (zf_ref, zb_ref)
    row_chunk = lax.broadcasted_iota(jnp.int32, (tile, dk), 0) // chunk

    def prep(t, carry):
        r0 = pl.multiple_of(t * tile, tile)
        rows = pl.ds(r0, tile)
        q = q_ref[0, rows, :].astype(F32)
        v = i_ref[0, rows, :]
        vt = v.astype(F32).T.astype(BF16)
        vt_s[:, rows] = vt
        for d in range(2):
            z = z_refs[d][0, rows, :].astype(F32)
            lb = lbs[d]
            f = lb + (1.0 - lb) * jax.nn.sigmoid(z)
            k = 1.0 - f
            lf = jnp.log(f)
            hi = lf.astype(BF16)
            lo = (lf - hi.astype(F32)).astype(BF16)
            x2 = jnp.concatenate([hi, lo], axis=1)
            r = jnp.dot(mats[d], x2, preferred_element_type=F32)
            r = r[:, :dk] + r[:, dk:]
            a1 = r[0:tile]
            bb = r[tile:2 * tile]
            a3 = r[2 * tile:3 * tile]
            tot = r[3 * tile:3 * tile + cpt]
            qa_s[d, rows, :] = (q * jnp.exp(a1)).astype(BF16)
            ka_s[d, rows, :] = (k * jnp.exp(-a1)).astype(BF16)
            qi_s[d, rows, :] = (q * jnp.exp(bb)).astype(BF16)
            ke = (k * jnp.exp(a3)).astype(BF16)
            dec_s[d, pl.ds(pl.multiple_of(t * cpt, cpt), cpt), :] = jnp.exp(tot)
            zero = jnp.zeros_like(ke)
            for j in range(0, cpt, 2):
                rhs = jnp.concatenate([jnp.where(row_chunk == j, ke, zero),
                                       jnp.where(row_chunk == j + 1, ke, zero)], axis=1)
                ut = jnp.dot(vt, rhs, preferred_element_type=F32)
                st_s[d, t * cpt + j] = ut[:, :dk]
                st_s[d, t * cpt + j + 1] = ut[:, dk:]
        return carry

    lax.fori_loop(0, n_tiles, prep, 0)

    def scan(n, carry):
        s_f, s_b = carry
        nb = n_chunks - 1 - n
        u_f = st_s[0, n]
        st_s[0, n] = s_f
        s_f = dec_s[0, pl.ds(n, 1), :] * s_f + u_f
        u_b = st_s[1, nb]
        st_s[1, nb] = s_b
        s_b = dec_s[1, pl.ds(nb, 1), :] * s_b + u_b
        return s_f, s_b

    zero_state = jnp.zeros((dk, dk), F32)
    lax.fori_loop(0, n_chunks, scan, (zero_state, zero_state))

    tt = lax.broadcasted_iota(jnp.int32, (tile, tile), 0)
    uu = lax.broadcasted_iota(jnp.int32, (tile, tile), 1)
    same = (tt // chunk) == (uu // chunk)
    causal = (same & (uu <= tt), same & (uu >= tt))
    ng = ng_ref[0]

    def emit(t, carry):
        r0 = pl.multiple_of(t * tile, tile)
        rows = pl.ds(r0, tile)
        v = i_ref[0, rows, :]
        o = jnp.zeros((tile, dk), F32)
        for d in range(2):
            sc = lax.dot_general(qa_s[d, rows, :], ka_s[d, rows, :], NT_DIMS,
                                 preferred_element_type=F32)
            pm = jnp.where(causal[d], sc, 0.0).astype(BF16)
            o = o + jnp.dot(pm, v, preferred_element_type=F32)
            qi = qi_s[d, rows, :]
            parts = []
            for j in range(cpt):
                st = st_s[d, t * cpt + j].astype(BF16)
                parts.append(lax.dot_general(qi[j * chunk:(j + 1) * chunk], st, NT_DIMS,
                                             preferred_element_type=F32))
            o = o + jnp.concatenate(parts, axis=0)
        on = o * lax.rsqrt(jnp.mean(o * o, axis=-1, keepdims=True) + RMS_EPS) * ng
        g = g_ref[0, rows, :].astype(F32)
        o_ref[0, rows, :] = (on * (g * jax.nn.sigmoid(g))).astype(o_ref.dtype)
        return carry

    lax.fori_loop(0, n_tiles, emit, 0)


def _hgrn_branch(proj, lb_logits, norm_g, layer, width):
    bsz, seq, _ = proj.shape
    dk = HG_HEAD_DIM
    heads = width // dk
    depth = lb_logits.shape[1]
    n_chunks = seq // HG_CHUNK

    def col(group):
        return pl.BlockSpec((1, seq, dk), lambda b, h: (b, 0, group * heads + h))

    return pl.pallas_call(
        functools.partial(_hgrn_kernel, layer=layer),
        grid=(bsz, heads),
        in_specs=[col(0), col(1), col(2), col(3), col(4),
                  pl.BlockSpec((2, depth, dk), lambda b, h: (0, 0, h)),
                  pl.BlockSpec((1, 1, dk), lambda b, h: (layer, 0, 0))],
        out_specs=pl.BlockSpec((1, seq, dk), lambda b, h: (b, 0, h)),
        out_shape=jax.ShapeDtypeStruct((bsz, seq, width), BF16),
        scratch_shapes=[pltpu.VMEM((2, seq, dk), BF16),
                        pltpu.VMEM((2, seq, dk), BF16),
                        pltpu.VMEM((2, seq, dk), BF16),
                        pltpu.VMEM((2, seq, dk), BF16),
                        pltpu.VMEM((dk, seq), BF16),
                        pltpu.VMEM((2, n_chunks, dk, dk), F32),
                        pltpu.VMEM((2, n_chunks, dk), F32)],
        compiler_params=_params(2),
        name="hgrn2_branch",
    )(proj, proj, proj, proj, proj, lb_logits, norm_g.reshape(norm_g.shape[0], 1, dk))


def _na_bias_table(rpb_l, rows):
    kr_n = min(NA_ROWS_MAX, rows)
    half = kr_n // 2
    cases_r = list(range(half)) + [half] + list(range(rows - half + 1, rows))
    r = np.array(cases_r)
    rs = np.clip(r - half, 0, rows - kr_n)
    row_off = rs[:, None] + np.arange(kr_n)[None, :] - r[:, None] + NA_ROWS_MAX - 1
    qc = np.arange(GRID_W)
    cs = np.clip(qc - NA_KC // 2, 0, GRID_W - NA_KC)
    kc = np.arange(GRID_W)
    valid = (kc[None, :] >= cs[:, None]) & (kc[None, :] < cs[:, None] + NA_KC)
    col_off = np.clip(kc[None, :] - qc[:, None] + NA_KC - 1, 0, 2 * NA_KC - 2)
    bias = rpb_l.astype(F32)[:, row_off[:, None, :, None], col_off[None, :, None, :]]
    bias = jnp.where(valid[None, None, :, None, :], bias, MASK_VALUE)
    return bias.reshape(rpb_l.shape[0], len(cases_r), GRID_W, kr_n * GRID_W)


def _na_kernel(q_ref, k_ref, v_ref, g_ref, bias_ref, o_ref, *, rows):
    w = GRID_W
    kr_n = min(NA_ROWS_MAX, rows)
    half = kr_n // 2
    win = kr_n * w
    scale = NA_HEAD_DIM ** -0.5
    lane = lax.broadcasted_iota(jnp.int32, (w, LANES), 1)
    head0 = lane < NA_HEAD_DIM

    def body(r, carry):
        rs = jnp.clip(r - half, 0, rows - kr_n)
        case = jnp.where(r < half, r, jnp.where(r > rows - half, r - (rows - kr_n), half))
        qrows = pl.ds(pl.multiple_of(r * w, w), w)
        krows = pl.ds(pl.multiple_of(rs * w, w), win)
        q = q_ref[0, qrows, :]
        kw = k_ref[0, krows, :]
        vw = v_ref[0, krows, :]
        outs = []
        for h in range(2):
            qh = jnp.where(head0 if h == 0 else ~head0, q, jnp.zeros_like(q))
            s = lax.dot_general(qh, kw, NT_DIMS, preferred_element_type=F32) * scale + bias_ref[h, case]
            m = jnp.max(s, axis=-1, keepdims=True)
            p = jnp.exp(s - m)
            l = jnp.sum(p, axis=-1, keepdims=True)
            outs.append(jnp.dot(p.astype(BF16), vw, preferred_element_type=F32) / l)
        o = jnp.where(head0, outs[0], outs[1])
        g = g_ref[0, qrows, :].astype(F32)
        o_ref[0, qrows, :] = (o * (g * jax.nn.sigmoid(g))).astype(o_ref.dtype)
        return carry

    lax.fori_loop(0, rows, body, 0)


def _na_branch(proj, bias, col0, width):
    bsz, seq, _ = proj.shape
    rows = seq // GRID_W
    pairs = width // LANES
    n_cases, win = bias.shape[1], bias.shape[3]

    def col(group):
        return pl.BlockSpec((1, seq, LANES), lambda p, b: (b, 0, col0 + group * pairs + p))

    return pl.pallas_call(
        functools.partial(_na_kernel, rows=rows),
        grid=(pairs, bsz),
        in_specs=[col(0), col(1), col(2), col(3),
                  pl.BlockSpec((2, n_cases, GRID_W, win), lambda p, b: (p, 0, 0, 0))],
        out_specs=pl.BlockSpec((1, seq, LANES), lambda p, b: (b, 0, p)),
        out_shape=jax.ShapeDtypeStruct((bsz, seq, width), BF16),
        compiler_params=_params(2),
        name="neighbourhood_attention",
    )(proj, proj, proj, proj, bias)


def _outproj_kernel(yh_ref, yn_ref, x_ref, mod_ref, w_ref, lg_ref, lb_ref, o_ref, *, d, alpha):
    hw = yh_ref.shape[1]
    y = jnp.dot(yh_ref[...], w_ref[0, pl.ds(0, hw), :], preferred_element_type=F32)
    y = y + jnp.dot(yn_ref[...], w_ref[0, pl.ds(hw, yn_ref.shape[1]), :], preferred_element_type=F32)
    gate = mod_ref[0, 0][:, 2 * d:3 * d]
    z = alpha * x_ref[...] + gate * y
    mu = jnp.mean(z, axis=-1, keepdims=True)
    zc = z - mu
    var = jnp.mean(zc * zc, axis=-1, keepdims=True)
    o_ref[...] = zc * lax.rsqrt(var + LN_EPS) * lg_ref[0] + lb_ref[0]


def _out_projection(y_hg2, y_na2, x2, mod4, w_out_bf, ln_g, ln_b, layer, seq, tm, alpha):
    m, d = x2.shape
    hw, nw = y_hg2.shape[1], y_na2.shape[1]
    return pl.pallas_call(
        functools.partial(_outproj_kernel, d=d, alpha=alpha),
        grid=(m // tm,),
        in_specs=[pl.BlockSpec((tm, hw), lambda i: (i, 0)),
                  pl.BlockSpec((tm, nw), lambda i: (i, 0)),
                  pl.BlockSpec((tm, d), lambda i: (i, 0)),
                  pl.BlockSpec((1, 1, 1, 3 * d), lambda i: (layer, (i * tm) // seq, 0, 0)),
                  pl.BlockSpec((1, hw + nw, d), lambda i: (layer, 0, 0)),
                  pl.BlockSpec((1, 1, d), lambda i: (layer, 0, 0)),
                  pl.BlockSpec((1, 1, d), lambda i: (layer, 0, 0))],
        out_specs=pl.BlockSpec((tm, d), lambda i: (i, 0)),
        out_shape=jax.ShapeDtypeStruct((m, d), F32),
        compiler_params=_params(1),
        name="out_projection_deepnorm_ln",
    )(y_hg2, y_na2, x2, mod4, w_out_bf, ln_g.reshape(-1, 1, d), ln_b.reshape(-1, 1, d))


def kernel(x, c, ada_w, ada_b, w_in, lb_logits, hg_norm_g, rpb, w_out, ln_g, ln_b):
    bsz, seq, d = x.shape
    depth = w_in.shape[0]
    mix = w_out.shape[1]
    hg_width = mix // 2
    na_width = mix - hg_width
    rows = seq // GRID_W
    alpha = (2 * depth) ** 0.25
    tm = 1024
    tn = 1536

    mod = _ada_modulation(c, ada_w, ada_b)
    mod4 = mod.reshape(depth, bsz, 1, 3 * d)
    w_in_bf = w_in.astype(BF16)
    w_out_bf = w_out.astype(BF16)
    na_col0 = 5 * hg_width // LANES

    x2 = x.reshape(bsz * seq, d)
    for layer in range(depth):
        proj = _in_projection(x2, mod4, w_in_bf, layer, seq, tm, tn).reshape(bsz, seq, -1)
        y_hg = _hgrn_branch(proj, lb_logits, hg_norm_g, layer, hg_width)
        bias = _na_bias_table(rpb[layer], rows)
        y_na = _na_branch(proj, bias, na_col0, na_width)
        x2 = _out_projection(y_hg.reshape(bsz * seq, hg_width), y_na.reshape(bsz * seq, na_width),
                             x2, mod4, w_out_bf, ln_g, ln_b, layer, seq, tm, alpha)
    return x2.reshape(bsz, seq, d)
```

```python
import functools

import numpy as np
import jax
import jax.numpy as jnp
from jax import lax
from jax.experimental import pallas as pl
from jax.experimental.pallas import tpu as pltpu

GRID_W = 64
HG_HEAD_DIM = 128
HG_CHUNK = 32
NA_HEAD_DIM = 64
NA_ROWS_MAX = 8
NA_KC = 16
LN_EPS = 1e-5
RMS_EPS = 1e-6
MASK_VALUE = -1e30

LANES = 128
MXU_DIM = 256
VMEM_LIMIT_BYTES = 56 * 1024 * 1024

F32 = jnp.float32
BF16 = jnp.bfloat16
NT_DIMS = (((1,), (1,)), ((), ()))


def _params(n_axes):
    return pltpu.CompilerParams(dimension_semantics=("arbitrary",) * n_axes,
                                vmem_limit_bytes=VMEM_LIMIT_BYTES)


def _ada_kernel(c_ref, w_ref, b_ref, o_ref):
    c = c_ref[...]
    a = c * jax.nn.sigmoid(c)
    o_ref[0] = jnp.dot(a, w_ref[0], precision=lax.Precision.HIGHEST,
                       preferred_element_type=F32) + b_ref[0]


def _ada_modulation(c, ada_w, ada_b):
    depth, d, cols = ada_w.shape
    bsz = c.shape[0]
    nj = cols // d
    return pl.pallas_call(
        _ada_kernel,
        grid=(depth, nj),
        in_specs=[pl.BlockSpec((bsz, d), lambda l, j: (0, 0)),
                  pl.BlockSpec((1, d, d), lambda l, j: (l, 0, j)),
                  pl.BlockSpec((1, 1, d), lambda l, j: (l, 0, j))],
        out_specs=pl.BlockSpec((1, bsz, d), lambda l, j: (l, 0, j)),
        out_shape=jax.ShapeDtypeStruct((depth, bsz, cols), F32),
        compiler_params=_params(2),
        name="ada_modulation",
    )(c, ada_w, ada_b.reshape(depth, 1, cols))


def _inproj_kernel(x_ref, mod_ref, w_ref, o_ref, h_ref, *, d):
    @pl.when(pl.program_id(1) == 0)
    def _():
        x = x_ref[...]
        mu = jnp.mean(x, axis=-1, keepdims=True)
        xc = x - mu
        var = jnp.mean(xc * xc, axis=-1, keepdims=True)
        hn = xc * lax.rsqrt(var + LN_EPS)
        m = mod_ref[0, 0]
        h_ref[...] = (hn * (1.0 + m[:, d:2 * d]) + m[:, 0:d]).astype(BF16)

    o_ref[...] = jnp.dot(h_ref[...], w_ref[0], preferred_element_type=F32).astype(o_ref.dtype)


def _in_projection(x2, mod4, w_in_bf, layer, seq, tm, tn):
    m, d = x2.shape
    n = w_in_bf.shape[2]
    return pl.pallas_call(
        functools.partial(_inproj_kernel, d=d),
        grid=(m // tm, n // tn),
        in_specs=[pl.BlockSpec((tm, d), lambda i, j: (i, 0)),
                  pl.BlockSpec((1, 1, 1, 3 * d), lambda i, j: (layer, (i * tm) // seq, 0, 0)),
                  pl.BlockSpec((1, d, tn), lambda i, j: (layer, 0, j))],
        out_specs=pl.BlockSpec((tm, tn), lambda i, j: (i, j)),
        out_shape=jax.ShapeDtypeStruct((m, n), BF16),
        scratch_shapes=[pltpu.VMEM((tm, d), BF16)],
        compiler_params=_params(2),
        name="ln_in_projection",
    )(x2, mod4, w_in_bf)


def _chunk_sum_matrices(tile, chunk):
    t = lax.broadcasted_iota(jnp.int32, (tile, tile), 0)
    u = lax.broadcasted_iota(jnp.int32, (tile, tile), 1)
    same = (t // chunk) == (u // chunk)
    start = (t // chunk) * chunk
    n_chunks = tile // chunk
    sel_rows = 16
    cj = lax.broadcasted_iota(jnp.int32, (sel_rows, tile), 0)
    cu = lax.broadcasted_iota(jnp.int32, (sel_rows, tile), 1)
    csel = jnp.where((cu // chunk == cj) & (cj < n_chunks), 1.0, 0.0)

    def stack(incl, incl_mid, rest):
        f = lambda m: jnp.where(same & m, 1.0, 0.0)
        return jnp.concatenate([f(incl) - f(incl_mid), f(incl), f(rest), csel], axis=0).astype(BF16)

    mid_f = start + (chunk // 2 - 1)
    fw = stack(u <= t, u <= mid_f, u > t)
    mid_b = start + chunk // 2
    bw = stack(u >= t, u >= mid_b, u < t)
    return fw, bw


def _hgrn_kernel(q_ref, i_ref, zf_ref, zb_ref, g_ref, lbl_ref, ng_ref, o_ref,
                 qa_s, ka_s, qi_s, ke_s, vt_s, st_s, dec_s, *, layer):
    seq = q_ref.shape[1]
    dk = q_ref.shape[2]
    tile = MXU_DIM
    chunk = HG_CHUNK
    cpt = tile // chunk
    n_tiles = seq // tile
    n_chunks = seq // chunk

    lg = lbl_ref[...]
    e = jnp.exp(lg - jnp.max(lg, axis=1, keepdims=True))
    p = e / jnp.sum(e, axis=1, keepdims=True)
    if layer == 0:
        lbs = [jnp.zeros((1, dk), F32)] * 2
    else:
        lbs = [jnp.sum(p[d, 1:layer + 1, :], axis=0, keepdims=True) for d in range(2)]

    m_fw, m_bw = _chunk_sum_matrices(tile, chunk)
    mats = (m_fw, m_bw)
    z_refs = (zf_ref, zb_ref)
    row_chunk = lax.broadcasted_iota(jnp.int32, (tile, dk), 0) // chunk

    def prep(t, carry):
        r0 = pl.multiple_of(t * tile, tile)
        rows = pl.ds(r0, tile)
        q = q_ref[0, rows, :].astype(F32)
        v = i_ref[0, rows, :]
        vt = v.astype(F32).T.astype(BF16)
        vt_s[:, rows] = vt
        for d in range(2):
            z = z_refs[d][0, rows, :].astype(F32)
            lb = lbs[d]
            f = lb + (1.0 - lb) * jax.nn.sigmoid(z)
            k = 1.0 - f
            lf = jnp.log(f)
            hi = lf.astype(BF16)
            lo = (lf - hi.astype(F32)).astype(BF16)
            x2 = jnp.concatenate([hi, lo], axis=1)
            r = jnp.dot(mats[d], x2, preferred_element_type=F32)
            r = r[:, :dk] + r[:, dk:]
            a1 = r[0:tile]
            bb = r[tile:2 * tile]
            a3 = r[2 * tile:3 * tile]
            tot = r[3 * tile:3 * tile + cpt]
            qa_s[d, rows, :] = (q * jnp.exp(a1)).astype(BF16)
            ka_s[d, rows, :] = (k * jnp.exp(-a1)).astype(BF16)
            qi_s[d, rows, :] = (q * jnp.exp(bb)).astype(BF16)
            ke = (k * jnp.exp(a3)).astype(BF16)
            dec_s[d, pl.ds(pl.multiple_of(t * cpt, cpt), cpt), :] = jnp.exp(tot)
            zero = jnp.zeros_like(ke)
            for j in range(0, cpt, 2):
                rhs = jnp.concatenate([jnp.where(row_chunk == j, ke, zero),
                                       jnp.where(row_chunk == j + 1, ke, zero)], axis=1)
                ut = jnp.dot(vt, rhs, preferred_element_type=F32)
                st_s[d, t * cpt + j] = ut[:, :dk]
                st_s[d, t * cpt + j + 1] = ut[:, dk:]
        return carry

    lax.fori_loop(0, n_tiles, prep, 0)

    def scan(n, carry):
        s_f, s_b = carry
        nb = n_chunks - 1 - n
        u_f = st_s[0, n]
        st_s[0, n] = s_f
        s_f = dec_s[0, pl.ds(n, 1), :] * s_f + u_f
        u_b = st_s[1, nb]
        st_s[1, nb] = s_b
        s_b = dec_s[1, pl.ds(nb, 1), :] * s_b + u_b
        return s_f, s_b

    zero_state = jnp.zeros((dk, dk), F32)
    lax.fori_loop(0, n_chunks, scan, (zero_state, zero_state))

    tt = lax.broadcasted_iota(jnp.int32, (tile, tile), 0)
    uu = lax.broadcasted_iota(jnp.int32, (tile, tile), 1)
    same = (tt // chunk) == (uu // chunk)
    causal = (same & (uu <= tt), same & (uu >= tt))
    ng = ng_ref[0]

    def emit(t, carry):
        r0 = pl.multiple_of(t * tile, tile)
        rows = pl.ds(r0, tile)
        v = i_ref[0, rows, :]
        o = jnp.zeros((tile, dk), F32)
        for d in range(2):
            sc = lax.dot_general(qa_s[d, rows, :], ka_s[d, rows, :], NT_DIMS,
                                 preferred_element_type=F32)
            pm = jnp.where(causal[d], sc, 0.0).astype(BF16)
            o = o + jnp.dot(pm, v, preferred_element_type=F32)
            qi = qi_s[d, rows, :]
            parts = []
            for j in range(cpt):
                st = st_s[d, t * cpt + j].astype(BF16)
                parts.append(lax.dot_general(qi[j * chunk:(j + 1) * chunk], st, NT_DIMS,
                                             preferred_element_type=F32))
            o = o + jnp.concatenate(parts, axis=0)
        on = o * lax.rsqrt(jnp.mean(o * o, axis=-1, keepdims=True) + RMS_EPS) * ng
        g = g_ref[0, rows, :].astype(F32)
        o_ref[0, rows, :] = (on * (g * jax.nn.sigmoid(g))).astype(o_ref.dtype)
        return carry

    lax.fori_loop(0, n_tiles, emit, 0)


def _hgrn_branch(proj, lb_logits, norm_g, layer, width):
    bsz, seq, _ = proj.shape
    dk = HG_HEAD_DIM
    heads = width // dk
    depth = lb_logits.shape[1]
    n_chunks = seq // HG_CHUNK

    def col(group):
        return pl.BlockSpec((1, seq, dk), lambda b, h: (b, 0, group * heads + h))

    return pl.pallas_call(
        functools.partial(_hgrn_kernel, layer=layer),
        grid=(bsz, heads),
        in_specs=[col(0), col(1), col(2), col(3), col(4),
                  pl.BlockSpec((2, depth, dk), lambda b, h: (0, 0, h)),
                  pl.BlockSpec((1, 1, dk), lambda b, h: (layer, 0, 0))],
        out_specs=pl.BlockSpec((1, seq, dk), lambda b, h: (b, 0, h)),
        out_shape=jax.ShapeDtypeStruct((bsz, seq, width), BF16),
        scratch_shapes=[pltpu.VMEM((2, seq, dk), BF16),
                        pltpu.VMEM((2, seq, dk), BF16),
                        pltpu.VMEM((2, seq, dk), BF16),
                        pltpu.VMEM((2, seq, dk), BF16),
                        pltpu.VMEM((dk, seq), BF16),
                        pltpu.VMEM((2, n_chunks, dk, dk), F32),
                        pltpu.VMEM((2, n_chunks, dk), F32)],
        compiler_params=_params(2),
        name="hgrn2_branch",
    )(proj, proj, proj, proj, proj, lb_logits, norm_g.reshape(norm_g.shape[0], 1, dk))


def _na_row_cases(rows):
    kr_n = min(NA_ROWS_MAX, rows)
    half = kr_n // 2
    case_rows = np.array(list(range(half)) + [half] + list(range(rows - half + 1, rows)))
    rs = np.clip(case_rows - half, 0, rows - kr_n)
    return rs - case_rows + NA_ROWS_MAX - 1


def _na_bias_table(rpb_l, rows):
    kr_n = min(NA_ROWS_MAX, rows)
    n_off = 2 * NA_KC - 1
    qc = np.arange(GRID_W)
    cs = np.clip(qc - NA_KC // 2, 0, GRID_W - NA_KC)
    kc = np.arange(GRID_W)
    valid = (kc[None, :] >= cs[:, None]) & (kc[None, :] < cs[:, None] + NA_KC)
    col_off = kc[None, :] - qc[:, None] + NA_KC - 1
    onehot = (col_off[None] == np.arange(n_off)[:, None, None]) & valid[None]
    heads = rpb_l.shape[0]
    rpb_p = rpb_l.astype(F32).reshape(heads // 2, 2, rpb_l.shape[1], n_off)
    tz = jnp.einsum('phro,oqk->prkhq', rpb_p, jnp.asarray(onehot, F32), precision=lax.Precision.HIGHEST)
    tz = jnp.where(jnp.asarray(valid.T)[None, None, :, None, :], tz, MASK_VALUE)
    slabs = [tz[:, b0:b0 + kr_n].reshape(heads // 2, kr_n * GRID_W, 2 * GRID_W) for b0 in _na_row_cases(rows)]
    return jnp.stack(slabs, axis=1)


def _na_kernel(q_ref, k_ref, v_ref, g_ref, bias_ref, o_ref, vt_s, *, rows):
    w = GRID_W
    seq = rows * w
    kr_n = min(NA_ROWS_MAX, rows)
    half = kr_n // 2
    win = kr_n * w
    tile = MXU_DIM
    scale = NA_HEAD_DIM ** -0.5
    head0 = lax.broadcasted_iota(jnp.int32, (w, LANES), 1) < NA_HEAD_DIM

    for t in range(seq // tile):
        src0 = t * tile
        vt_s[0, :, src0:src0 + tile] = v_ref[0, src0:src0 + tile, :].astype(F32).T.astype(BF16)
        src1 = min(src0 + w, seq - tile)
        vt_s[1, :, src1 - w:src1 - w + tile] = v_ref[0, src1:src1 + tile, :].astype(F32).T.astype(BF16)

    def body(r, carry):
        rs = jnp.clip(r - half, 0, rows - kr_n)
        case = jnp.where(r < half, r, jnp.where(r > rows - half, r - (rows - kr_n), half))
        qrows = pl.ds(pl.multiple_of(r * w, w), w)
        krows = pl.ds(pl.multiple_of(rs * w, w), win)
        par = rs % 2
        vcols = pl.ds(pl.multiple_of((rs - par) * w, 2 * w), win)
        q = q_ref[0, qrows, :] * scale
        zq = jnp.zeros_like(q)
        qbd = jnp.concatenate([jnp.where(head0, q, zq), jnp.where(head0, zq, q)], axis=0)
        st = lax.dot_general(k_ref[0, krows, :], qbd, NT_DIMS, preferred_element_type=F32)
        st = st + bias_ref[0, case]
        m = jnp.max(st, axis=0, keepdims=True)
        pt = jnp.exp(st - m)
        l = jnp.sum(pt, axis=0, keepdims=True)
        ot = jnp.dot(vt_s[par, :, vcols], pt.astype(BF16), preferred_element_type=F32) / l
        o2 = ot.T
        o = jnp.where(head0, o2[0:w], o2[w:2 * w])
        g = g_ref[0, qrows, :].astype(F32)
        o_ref[0, qrows, :] = (o * (g * jax.nn.sigmoid(g))).astype(o_ref.dtype)
        return carry

    lax.fori_loop(0, rows, body, 0, unroll=8)


def _na_branch(proj, bias, col0, width):
    bsz, seq, _ = proj.shape
    rows = seq // GRID_W
    pairs = width // LANES
    n_cases, win = bias.shape[1], bias.shape[2]

    def col(group):
        return pl.BlockSpec((1, seq, LANES), lambda p, b: (b, 0, col0 + group * pairs + p))

    return pl.pallas_call(
        functools.partial(_na_kernel, rows=rows),
        grid=(pairs, bsz),
        in_specs=[col(0), col(1), col(2), col(3),
                  pl.BlockSpec((1, n_cases, win, LANES), lambda p, b: (p, 0, 0, 0))],
        out_specs=pl.BlockSpec((1, seq, LANES), lambda p, b: (b, 0, p)),
        out_shape=jax.ShapeDtypeStruct((bsz, seq, width), BF16),
        scratch_shapes=[pltpu.VMEM((2, LANES, seq), BF16)],
        compiler_params=_params(2),
        name="neighbourhood_attention",
    )(proj, proj, proj, proj, bias)


def _outproj_kernel(yh_ref, yn_ref, x_ref, mod_ref, w_ref, lg_ref, lb_ref, o_ref, *, d, alpha):
    hw = yh_ref.shape[1]
    y = jnp.dot(yh_ref[...], w_ref[0, pl.ds(0, hw), :], preferred_element_type=F32)
    y = y + jnp.dot(yn_ref[...], w_ref[0, pl.ds(hw, yn_ref.shape[1]), :], preferred_element_type=F32)
    gate = mod_ref[0, 0][:, 2 * d:3 * d]
    z = alpha * x_ref[...] + gate * y
    mu = jnp.mean(z, axis=-1, keepdims=True)
    zc = z - mu
    var = jnp.mean(zc * zc, axis=-1, keepdims=True)
    o_ref[...] = zc * lax.rsqrt(var + LN_EPS) * lg_ref[0] + lb_ref[0]


def _out_projection(y_hg2, y_na2, x2, mod4, w_out_bf, ln_g, ln_b, layer, seq, tm, alpha):
    m, d = x2.shape
    hw, nw = y_hg2.shape[1], y_na2.shape[1]
    return pl.pallas_call(
        functools.partial(_outproj_kernel, d=d, alpha=alpha),
        grid=(m // tm,),
        in_specs=[pl.BlockSpec((tm, hw), lambda i: (i, 0)),
                  pl.BlockSpec((tm, nw), lambda i: (i, 0)),
                  pl.BlockSpec((tm, d), lambda i: (i, 0)),
                  pl.BlockSpec((1, 1, 1, 3 * d), lambda i: (layer, (i * tm) // seq, 0, 0)),
                  pl.BlockSpec((1, hw + nw, d), lambda i: (layer, 0, 0)),
                  pl.BlockSpec((1, 1, d), lambda i: (layer, 0, 0)),
                  pl.BlockSpec((1, 1, d), lambda i: (layer, 0, 0))],
        out_specs=pl.BlockSpec((tm, d), lambda i: (i, 0)),
        out_shape=jax.ShapeDtypeStruct((m, d), F32),
        compiler_params=_params(1),
        name="out_projection_deepnorm_ln",
    )(y_hg2, y_na2, x2, mod4, w_out_bf, ln_g.reshape(-1, 1, d), ln_b.reshape(-1, 1, d))


def kernel(x, c, ada_w, ada_b, w_in, lb_logits, hg_norm_g, rpb, w_out, ln_g, ln_b):
    bsz, seq, d = x.shape
    depth = w_in.shape[0]
    mix = w_out.shape[1]
    hg_width = mix // 2
    na_width = mix - hg_width
    rows = seq // GRID_W
    alpha = (2 * depth) ** 0.25
    tm = 1024
    tn = 1536

    mod = _ada_modulation(c, ada_w, ada_b)
    mod4 = mod.reshape(depth, bsz, 1, 3 * d)
    w_in_bf = w_in.astype(BF16)
    w_out_bf = w_out.astype(BF16)
    na_col0 = 5 * hg_width // LANES

    x2 = x.reshape(bsz * seq, d)
    for layer in range(depth):
        proj = _in_projection(x2, mod4, w_in_bf, layer, seq, tm, tn).reshape(bsz, seq, -1)
        y_hg = _hgrn_branch(proj, lb_logits, hg_norm_g, layer, hg_width)
        bias = _na_bias_table(rpb[layer], rows)
        y_na = _na_branch(proj, bias, na_col0, na_width)
        x2 = _out_projection(y_hg.reshape(bsz * seq, hg_width), y_na.reshape(bsz * seq, na_width),
                             x2, mod4, w_out_bf, ln_g, ln_b, layer, seq, tm, alpha)
    return x2.reshape(bsz, seq, d)
```

```python
import functools

import numpy as np
import jax
import jax.numpy as jnp
from jax import lax
from jax.experimental import pallas as pl
from jax.experimental.pallas import tpu as pltpu

GRID_W = 64
HG_HEAD_DIM = 128
HG_CHUNK = 32
NA_HEAD_DIM = 64
NA_ROWS_MAX = 8
NA_KC = 16
LN_EPS = 1e-5
RMS_EPS = 1e-6
MASK_VALUE = -1e30

LANES = 128
MXU_DIM = 256
VMEM_LIMIT_BYTES = 56 * 1024 * 1024

F32 = jnp.float32
BF16 = jnp.bfloat16
NT_DIMS = (((1,), (1,)), ((), ()))


def _params(n_axes):
    return pltpu.CompilerParams(dimension_semantics=("arbitrary",) * n_axes,
                                vmem_limit_bytes=VMEM_LIMIT_BYTES)


def _ada_kernel(c_ref, w_ref, b_ref, o_ref):
    c = c_ref[...]
    a = c * jax.nn.sigmoid(c)
    o_ref[0] = jnp.dot(a, w_ref[0], precision=lax.Precision.HIGHEST,
                       preferred_element_type=F32) + b_ref[0]


def _ada_modulation(c, ada_w, ada_b):
    depth, d, cols = ada_w.shape
    bsz = c.shape[0]
    nj = cols // d
    return pl.pallas_call(
        _ada_kernel,
        grid=(depth, nj),
        in_specs=[pl.BlockSpec((bsz, d), lambda l, j: (0, 0)),
                  pl.BlockSpec((1, d, d), lambda l, j: (l, 0, j)),
                  pl.BlockSpec((1, 1, d), lambda l, j: (l, 0, j))],
        out_specs=pl.BlockSpec((1, bsz, d), lambda l, j: (l, 0, j)),
        out_shape=jax.ShapeDtypeStruct((depth, bsz, cols), F32),
        compiler_params=_params(2),
        name="ada_modulation",
    )(c, ada_w, ada_b.reshape(depth, 1, cols))


def _inproj_kernel(x_ref, mod_ref, w_ref, o_ref, h_ref, *, d):
    @pl.when(pl.program_id(1) == 0)
    def _():
        x = x_ref[...]
        mu = jnp.mean(x, axis=-1, keepdims=True)
        xc = x - mu
        var = jnp.mean(xc * xc, axis=-1, keepdims=True)
        hn = xc * lax.rsqrt(var + LN_EPS)
        m = mod_ref[0, 0]
        h_ref[...] = (hn * (1.0 + m[:, d:2 * d]) + m[:, 0:d]).astype(BF16)

    o_ref[...] = jnp.dot(h_ref[...], w_ref[0], preferred_element_type=F32).astype(o_ref.dtype)


def _in_projection(x2, mod4, w_in_bf, layer, seq, tm, tn):
    m, d = x2.shape
    n = w_in_bf.shape[2]
    return pl.pallas_call(
        functools.partial(_inproj_kernel, d=d),
        grid=(m // tm, n // tn),
        in_specs=[pl.BlockSpec((tm, d), lambda i, j: (i, 0)),
                  pl.BlockSpec((1, 1, 1, 3 * d), lambda i, j: (layer, (i * tm) // seq, 0, 0)),
                  pl.BlockSpec((1, d, tn), lambda i, j: (layer, 0, j))],
        out_specs=pl.BlockSpec((tm, tn), lambda i, j: (i, j)),
        out_shape=jax.ShapeDtypeStruct((m, n), BF16),
        scratch_shapes=[pltpu.VMEM((tm, d), BF16)],
        compiler_params=_params(2),
        name="ln_in_projection",
    )(x2, mod4, w_in_bf)


def _chunk_sum_matrices(tile, chunk):
    t = lax.broadcasted_iota(jnp.int32, (tile, tile), 0)
    u = lax.broadcasted_iota(jnp.int32, (tile, tile), 1)
    same = (t // chunk) == (u // chunk)
    n = tile // chunk
    cj = lax.broadcasted_iota(jnp.int32, (2 * n, tile), 0)
    cu = lax.broadcasted_iota(jnp.int32, (2 * n, tile), 1)
    is_total = cj < n
    in_chunk = (cu // chunk) == jnp.where(is_total, cj, cj - n)
    pos = cu % chunk

    def as_bf16(m):
        return jnp.where(m, 1.0, 0.0).astype(BF16)

    fw = (as_bf16(same & (u <= t)), as_bf16(in_chunk & (is_total | (pos <= chunk // 2 - 1))))
    bw = (as_bf16(same & (u >= t)), as_bf16(in_chunk & (is_total | (pos >= chunk // 2))))
    return fw, bw


def _rows_to_chunks(x, chunk):
    return jnp.concatenate([jnp.broadcast_to(x[j:j + 1], (chunk, x.shape[1])) for j in range(x.shape[0])],
                           axis=0)


def _hgrn_kernel(q_ref, i_ref, zf_ref, zb_ref, g_ref, lbl_ref, ng_ref, o_ref,
                 qa_s, ka_s, qi_s, u_s, st_s, dec_s, *, layer):
    seq = q_ref.shape[1]
    dk = q_ref.shape[2]
    tile = MXU_DIM
    chunk = HG_CHUNK
    cpt = tile // chunk
    n_tiles = seq // tile
    n_chunks = seq // chunk

    lg = lbl_ref[...]
    e = jnp.exp(lg - jnp.max(lg, axis=1, keepdims=True))
    p = e / jnp.sum(e, axis=1, keepdims=True)
    if layer == 0:
        lbs = [jnp.zeros((1, dk), F32)] * 2
    else:
        lbs = [jnp.sum(p[d, 1:layer + 1, :], axis=0, keepdims=True) for d in range(2)]

    mats = _chunk_sum_matrices(tile, chunk)
    z_refs = (zf_ref, zb_ref)

    def tile_rows(t):
        return pl.ds(pl.multiple_of(t * tile, tile), tile)

    def chunk_cols(n):
        return pl.ds(pl.multiple_of(n * 2 * dk, 2 * dk), 2 * dk)

    def state_cols(n, d):
        return pl.ds(pl.multiple_of(n * 2 * dk + d * dk, dk), dk)

    col_chunk = lax.broadcasted_iota(jnp.int32, (dk, tile), 1) // chunk

    def prep(t, carry):
        rows = tile_rows(t)
        q = q_ref[0, rows, :].astype(F32)
        vt = i_ref[0, rows, :].astype(F32).T.astype(BF16)
        ke = []
        for d in range(2):
            z = z_refs[d][0, rows, :].astype(F32)
            lb = lbs[d]
            f = lb + (1.0 - lb) * jax.nn.sigmoid(z)
            k = 1.0 - f
            lf = jnp.log(f)
            hi = lf.astype(BF16)
            lo = (lf - hi.astype(F32)).astype(BF16)
            x2 = jnp.concatenate([hi, lo], axis=1)
            incl, per_chunk = mats[d]
            r = jnp.dot(incl, x2, preferred_element_type=F32)
            bb = r[:, :dk] + r[:, dk:]
            r = jnp.dot(per_chunk, x2, preferred_element_type=F32)
            r = r[:, :dk] + r[:, dk:]
            tot = r[0:cpt]
            a1 = bb - _rows_to_chunks(r[cpt:2 * cpt], chunk)
            a3 = _rows_to_chunks(tot, chunk) - bb
            lanes = slice(d * dk, (d + 1) * dk)
            qa_s[d, rows, :] = (q * jnp.exp(a1)).astype(BF16)
            ka_s[d, rows, :] = (k * jnp.exp(-a1)).astype(BF16)
            qi_s[rows, lanes] = (q * jnp.exp(bb)).astype(BF16)
            ke.append((k * jnp.exp(a3)).astype(BF16))
            dec_s[d, pl.ds(pl.multiple_of(t * cpt, cpt), cpt), :] = jnp.exp(tot)
        zero = jnp.zeros_like(vt)
        lhs = jnp.concatenate([jnp.where(col_chunk == j, vt, zero) for j in range(cpt)], axis=0)
        u = jnp.dot(lhs, jnp.concatenate(ke, axis=1), preferred_element_type=F32)
        for j in range(cpt):
            u_s[:, chunk_cols(t * cpt + j)] = u[j * dk:(j + 1) * dk]
        return carry

    lax.fori_loop(0, n_tiles, prep, 0, unroll=4)

    def scan(n, carry):
        s_f, s_b = carry
        nb = n_chunks - 1 - n
        st_s[:, state_cols(n, 0)] = s_f.astype(BF16)
        s_f = dec_s[0, pl.ds(n, 1), :] * s_f + u_s[:, state_cols(n, 0)]
        st_s[:, state_cols(nb, 1)] = s_b.astype(BF16)
        s_b = dec_s[1, pl.ds(nb, 1), :] * s_b + u_s[:, state_cols(nb, 1)]
        return s_f, s_b

    zero_state = jnp.zeros((dk, dk), F32)
    lax.fori_loop(0, n_chunks, scan, (zero_state, zero_state))

    tt = lax.broadcasted_iota(jnp.int32, (tile, tile), 0)
    uu = lax.broadcasted_iota(jnp.int32, (tile, tile), 1)
    same = (tt // chunk) == (uu // chunk)
    causal = (same & (uu <= tt), same & (uu >= tt))
    ng = ng_ref[0]

    def emit(t, carry):
        rows = tile_rows(t)
        v = i_ref[0, rows, :]
        pm = []
        for d in range(2):
            sc = lax.dot_general(qa_s[d, rows, :], ka_s[d, rows, :], NT_DIMS, preferred_element_type=F32)
            pm.append(jnp.where(causal[d], sc, 0.0).astype(BF16))
        o = jnp.dot(jnp.concatenate(pm, axis=1), jnp.concatenate([v, v], axis=0),
                    preferred_element_type=F32)
        qi = qi_s[rows, :]
        parts = []
        for j in range(cpt):
            parts.append(lax.dot_general(qi[j * chunk:(j + 1) * chunk], st_s[:, chunk_cols(t * cpt + j)],
                                         NT_DIMS, preferred_element_type=F32))
        o = o + jnp.concatenate(parts, axis=0)
        on = o * lax.rsqrt(jnp.mean(o * o, axis=-1, keepdims=True) + RMS_EPS) * ng
        g = g_ref[0, rows, :].astype(F32)
        o_ref[0, rows, :] = (on * (g * jax.nn.sigmoid(g))).astype(o_ref.dtype)
        return carry

    lax.fori_loop(0, n_tiles, emit, 0, unroll=4)


def _hgrn_branch(proj, lb_logits, norm_g, layer, width):
    bsz, seq, _ = proj.shape
    dk = HG_HEAD_DIM
    heads = width // dk
    depth = lb_logits.shape[1]
    n_chunks = seq // HG_CHUNK

    def col(group):
        return pl.BlockSpec((1, seq, dk), lambda b, h: (b, 0, group * heads + h))

    return pl.pallas_call(
        functools.partial(_hgrn_kernel, layer=layer),
        grid=(bsz, heads),
        in_specs=[col(0), col(1), col(2), col(3), col(4),
                  pl.BlockSpec((2, depth, dk), lambda b, h: (0, 0, h)),
                  pl.BlockSpec((1, 1, dk), lambda b, h: (layer, 0, 0))],
        out_specs=pl.BlockSpec((1, seq, dk), lambda b, h: (b, 0, h)),
        out_shape=jax.ShapeDtypeStruct((bsz, seq, width), BF16),
        scratch_shapes=[pltpu.VMEM((2, seq, dk), BF16),
                        pltpu.VMEM((2, seq, dk), BF16),
                        pltpu.VMEM((seq, 2 * dk), BF16),
                        pltpu.VMEM((dk, n_chunks * 2 * dk), F32),
                        pltpu.VMEM((dk, n_chunks * 2 * dk), BF16),
                        pltpu.VMEM((2, n_chunks, dk), F32)],
        compiler_params=_params(2),
        name="hgrn2_branch",
    )(proj, proj, proj, proj, proj, lb_logits, norm_g.reshape(norm_g.shape[0], 1, dk))


def _na_row_cases(rows):
    kr_n = min(NA_ROWS_MAX, rows)
    half = kr_n // 2
    case_rows = np.array(list(range(half)) + [half] + list(range(rows - half + 1, rows)))
    rs = np.clip(case_rows - half, 0, rows - kr_n)
    return rs - case_rows + NA_ROWS_MAX - 1


def _na_bias_table(rpb_l, rows):
    kr_n = min(NA_ROWS_MAX, rows)
    n_off = 2 * NA_KC - 1
    qc = np.arange(GRID_W)
    cs = np.clip(qc - NA_KC // 2, 0, GRID_W - NA_KC)
    kc = np.arange(GRID_W)
    valid = (kc[None, :] >= cs[:, None]) & (kc[None, :] < cs[:, None] + NA_KC)
    col_off = kc[None, :] - qc[:, None] + NA_KC - 1
    onehot = (col_off[None] == np.arange(n_off)[:, None, None]) & valid[None]
    heads = rpb_l.shape[0]
    rpb_p = rpb_l.astype(F32).reshape(heads // 2, 2, rpb_l.shape[1], n_off)
    tz = jnp.einsum('phro,oqk->prkhq', rpb_p, jnp.asarray(onehot, F32), precision=lax.Precision.HIGHEST)
    tz = jnp.where(jnp.asarray(valid.T)[None, None, :, None, :], tz, MASK_VALUE)
    slabs = [tz[:, b0:b0 + kr_n].reshape(heads // 2, kr_n * GRID_W, 2 * GRID_W) for b0 in _na_row_cases(rows)]
    return jnp.stack(slabs, axis=1)


def _na_kernel(q_ref, k_ref, v_ref, g_ref, bias_ref, o_ref, vt_s, *, rows):
    w = GRID_W
    seq = rows * w
    kr_n = min(NA_ROWS_MAX, rows)
    half = kr_n // 2
    win = kr_n * w
    tile = MXU_DIM
    scale = NA_HEAD_DIM ** -0.5
    head0 = lax.broadcasted_iota(jnp.int32, (w, LANES), 1) < NA_HEAD_DIM

    for t in range(seq // tile):
        src0 = t * tile
        vt_s[0, :, src0:src0 + tile] = v_ref[0, src0:src0 + tile, :].astype(F32).T.astype(BF16)
        src1 = min(src0 + w, seq - tile)
        vt_s[1, :, src1 - w:src1 - w + tile] = v_ref[0, src1:src1 + tile, :].astype(F32).T.astype(BF16)

    def body(r, carry):
        rs = jnp.clip(r - half, 0, rows - kr_n)
        case = jnp.where(r < half, r, jnp.where(r > rows - half, r - (rows - kr_n), half))
        qrows = pl.ds(pl.multiple_of(r * w, w), w)
        krows = pl.ds(pl.multiple_of(rs * w, w), win)
        par = rs % 2
        vcols = pl.ds(pl.multiple_of((rs - par) * w, 2 * w), win)
        q = q_ref[0, qrows, :] * scale
        zq = jnp.zeros_like(q)
        qbd = jnp.concatenate([jnp.where(head0, q, zq), jnp.where(head0, zq, q)], axis=0)
        st = lax.dot_general(k_ref[0, krows, :], qbd, NT_DIMS, preferred_element_type=F32)
        st = st + bias_ref[0, case]
        m = jnp.max(st, axis=0, keepdims=True)
        pt = jnp.exp(st - m)
        l = jnp.sum(pt, axis=0, keepdims=True)
        ot = jnp.dot(vt_s[par, :, vcols], pt.astype(BF16), preferred_element_type=F32) / l
        o2 = ot.T
        o = jnp.where(head0, o2[0:w], o2[w:2 * w])
        g = g_ref[0, qrows, :].astype(F32)
        o_ref[0, qrows, :] = (o * (g * jax.nn.sigmoid(g))).astype(o_ref.dtype)
        return carry

    lax.fori_loop(0, rows, body, 0, unroll=8)


def _na_branch(proj, bias, col0, width):
    bsz, seq, _ = proj.shape
    rows = seq // GRID_W
    pairs = width // LANES
    n_cases, win = bias.shape[1], bias.shape[2]

    def col(group):
        return pl.BlockSpec((1, seq, LANES), lambda p, b: (b, 0, col0 + group * pairs + p))

    return pl.pallas_call(
        functools.partial(_na_kernel, rows=rows),
        grid=(pairs, bsz),
        in_specs=[col(0), col(1), col(2), col(3),
                  pl.BlockSpec((1, n_cases, win, LANES), lambda p, b: (p, 0, 0, 0))],
        out_specs=pl.BlockSpec((1, seq, LANES), lambda p, b: (b, 0, p)),
        out_shape=jax.ShapeDtypeStruct((bsz, seq, width), BF16),
        scratch_shapes=[pltpu.VMEM((2, LANES, seq), BF16)],
        compiler_params=_params(2),
        name="neighbourhood_attention",
    )(proj, proj, proj, proj, bias)


def _outproj_kernel(yh_ref, yn_ref, x_ref, mod_ref, w_ref, lg_ref, lb_ref, o_ref, *, d, alpha):
    hw = yh_ref.shape[1]
    y = jnp.dot(yh_ref[...], w_ref[0, pl.ds(0, hw), :], preferred_element_type=F32)
    y = y + jnp.dot(yn_ref[...], w_ref[0, pl.ds(hw, yn_ref.shape[1]), :], preferred_element_type=F32)
    gate = mod_ref[0, 0][:, 2 * d:3 * d]
    z = alpha * x_ref[...] + gate * y
    mu = jnp.mean(z, axis=-1, keepdims=True)
    zc = z - mu
    var = jnp.mean(zc * zc, axis=-1, keepdims=True)
    o_ref[...] = zc * lax.rsqrt(var + LN_EPS) * lg_ref[0] + lb_ref[0]


def _out_projection(y_hg2, y_na2, x2, mod4, w_out_bf, ln_g, ln_b, layer, seq, tm, alpha):
    m, d = x2.shape
    hw, nw = y_hg2.shape[1], y_na2.shape[1]
    return pl.pallas_call(
        functools.partial(_outproj_kernel, d=d, alpha=alpha),
        grid=(m // tm,),
        in_specs=[pl.BlockSpec((tm, hw), lambda i: (i, 0)),
                  pl.BlockSpec((tm, nw), lambda i: (i, 0)),
                  pl.BlockSpec((tm, d), lambda i: (i, 0)),
                  pl.BlockSpec((1, 1, 1, 3 * d), lambda i: (layer, (i * tm) // seq, 0, 0)),
                  pl.BlockSpec((1, hw + nw, d), lambda i: (layer, 0, 0)),
                  pl.BlockSpec((1, 1, d), lambda i: (layer, 0, 0)),
                  pl.BlockSpec((1, 1, d), lambda i: (layer, 0, 0))],
        out_specs=pl.BlockSpec((tm, d), lambda i: (i, 0)),
        out_shape=jax.ShapeDtypeStruct((m, d), F32),
        compiler_params=_params(1),
        name="out_projection_deepnorm_ln",
    )(y_hg2, y_na2, x2, mod4, w_out_bf, ln_g.reshape(-1, 1, d), ln_b.reshape(-1, 1, d))


def kernel(x, c, ada_w, ada_b, w_in, lb_logits, hg_norm_g, rpb, w_out, ln_g, ln_b):
    bsz, seq, d = x.shape
    depth = w_in.shape[0]
    mix = w_out.shape[1]
    hg_width = mix // 2
    na_width = mix - hg_width
    rows = seq // GRID_W
    alpha = (2 * depth) ** 0.25
    tm = 1024
    tn = 1536

    mod = _ada_modulation(c, ada_w, ada_b)
    mod4 = mod.reshape(depth, bsz, 1, 3 * d)
    w_in_bf = w_in.astype(BF16)
    w_out_bf = w_out.astype(BF16)
    na_col0 = 5 * hg_width // LANES

    x2 = x.reshape(bsz * seq, d)
    for layer in range(depth):
        proj = _in_projection(x2, mod4, w_in_bf, layer, seq, tm, tn).reshape(bsz, seq, -1)
        y_hg = _hgrn_branch(proj, lb_logits, hg_norm_g, layer, hg_width)
        bias = _na_bias_table(rpb[layer], rows)
        y_na = _na_branch(proj, bias, na_col0, na_width)
        x2 = _out_projection(y_hg.reshape(bsz * seq, hg_width), y_na.reshape(bsz * seq, na_width),
                             x2, mod4, w_out_bf, ln_g, ln_b, layer, seq, tm, alpha)
    return x2.reshape(bsz, seq, d)
```

```python
import functools

import numpy as np
import jax
import jax.numpy as jnp
from jax import lax
from jax.experimental import pallas as pl
from jax.experimental.pallas import tpu as pltpu

GRID_W = 64
HG_HEAD_DIM = 128
HG_CHUNK = 32
NA_HEAD_DIM = 64
NA_ROWS_MAX = 8
NA_KC = 16
LN_EPS = 1e-5
RMS_EPS = 1e-6
MASK_VALUE = -1e30

LANES = 128
MXU_DIM = 256
VMEM_LIMIT_BYTES = 56 * 1024 * 1024

F32 = jnp.float32
BF16 = jnp.bfloat16
NT_DIMS = (((1,), (1,)), ((), ()))


def _params(n_axes):
    return pltpu.CompilerParams(dimension_semantics=("arbitrary",) * n_axes,
                                vmem_limit_bytes=VMEM_LIMIT_BYTES)


def _ada_kernel(c_ref, w_ref, b_ref, o_ref):
    c = c_ref[...]
    a = c * jax.nn.sigmoid(c)
    o_ref[0] = jnp.dot(a, w_ref[0], precision=lax.Precision.HIGHEST,
                       preferred_element_type=F32) + b_ref[0]


def _ada_modulation(c, ada_w, ada_b):
    depth, d, cols = ada_w.shape
    bsz = c.shape[0]
    nj = cols // d
    return pl.pallas_call(
        _ada_kernel,
        grid=(depth, nj),
        in_specs=[pl.BlockSpec((bsz, d), lambda l, j: (0, 0)),
                  pl.BlockSpec((1, d, d), lambda l, j: (l, 0, j)),
                  pl.BlockSpec((1, 1, d), lambda l, j: (l, 0, j))],
        out_specs=pl.BlockSpec((1, bsz, d), lambda l, j: (l, 0, j)),
        out_shape=jax.ShapeDtypeStruct((depth, bsz, cols), F32),
        compiler_params=_params(2),
        name="ada_modulation",
    )(c, ada_w, ada_b.reshape(depth, 1, cols))


def _inproj_kernel(x_ref, mod_ref, w_ref, o_ref, h_ref, *, d):
    @pl.when(pl.program_id(1) == 0)
    def _():
        x = x_ref[...]
        mu = jnp.mean(x, axis=-1, keepdims=True)
        xc = x - mu
        var = jnp.mean(xc * xc, axis=-1, keepdims=True)
        hn = xc * lax.rsqrt(var + LN_EPS)
        m = mod_ref[0, 0]
        h_ref[...] = (hn * (1.0 + m[:, d:2 * d]) + m[:, 0:d]).astype(BF16)

    o_ref[...] = jnp.dot(h_ref[...], w_ref[0], preferred_element_type=F32).astype(o_ref.dtype)


def _in_projection(x2, mod4, w_in_bf, layer, seq, tm, tn):
    m, d = x2.shape
    n = w_in_bf.shape[2]
    return pl.pallas_call(
        functools.partial(_inproj_kernel, d=d),
        grid=(m // tm, n // tn),
        in_specs=[pl.BlockSpec((tm, d), lambda i, j: (i, 0)),
                  pl.BlockSpec((1, 1, 1, 3 * d), lambda i, j: (layer, (i * tm) // seq, 0, 0)),
                  pl.BlockSpec((1, d, tn), lambda i, j: (layer, 0, j))],
        out_specs=pl.BlockSpec((tm, tn), lambda i, j: (i, j)),
        out_shape=jax.ShapeDtypeStruct((m, n), BF16),
        scratch_shapes=[pltpu.VMEM((tm, d), BF16)],
        compiler_params=_params(2),
        name="ln_in_projection",
    )(x2, mod4, w_in_bf)


def _chunk_sum_matrices(tile, chunk):
    t = lax.broadcasted_iota(jnp.int32, (tile, tile), 0)
    u = lax.broadcasted_iota(jnp.int32, (tile, tile), 1)
    same = (t // chunk) == (u // chunk)
    n = tile // chunk
    cj = lax.broadcasted_iota(jnp.int32, (2 * n, tile), 0)
    cu = lax.broadcasted_iota(jnp.int32, (2 * n, tile), 1)
    is_total = cj < n
    in_chunk = (cu // chunk) == jnp.where(is_total, cj, cj - n)
    pos = cu % chunk

    def as_bf16(m):
        return jnp.where(m, 1.0, 0.0).astype(BF16)

    fw = (as_bf16(same & (u <= t)), as_bf16(in_chunk & (is_total | (pos <= chunk // 2 - 1))))
    bw = (as_bf16(same & (u >= t)), as_bf16(in_chunk & (is_total | (pos >= chunk // 2))))
    return fw, bw


def _rows_to_chunks(x, chunk):
    return jnp.concatenate([jnp.broadcast_to(x[j:j + 1], (chunk, x.shape[1])) for j in range(x.shape[0])],
                           axis=0)


def _hgrn_kernel(q_ref, i_ref, zf_ref, zb_ref, g_ref, lbl_ref, ng_ref, o_ref,
                 qa_s, ka_s, qi_s, u_s, st_s, dec_s, *, layer):
    seq = q_ref.shape[1]
    dk = q_ref.shape[2]
    tile = MXU_DIM
    chunk = HG_CHUNK
    cpt = tile // chunk
    n_tiles = seq // tile
    n_chunks = seq // chunk

    lg = lbl_ref[...]
    e = jnp.exp(lg - jnp.max(lg, axis=1, keepdims=True))
    p = e / jnp.sum(e, axis=1, keepdims=True)
    if layer == 0:
        lbs = [jnp.zeros((1, dk), F32)] * 2
    else:
        lbs = [jnp.sum(p[d, 1:layer + 1, :], axis=0, keepdims=True) for d in range(2)]

    mats = _chunk_sum_matrices(tile, chunk)
    z_refs = (zf_ref, zb_ref)

    def tile_rows(t):
        return pl.ds(pl.multiple_of(t * tile, tile), tile)

    def chunk_cols(n):
        return pl.ds(pl.multiple_of(n * 2 * dk, 2 * dk), 2 * dk)

    def state_cols(n, d):
        return pl.ds(pl.multiple_of(n * 2 * dk + d * dk, dk), dk)

    col_chunk = lax.broadcasted_iota(jnp.int32, (dk, tile), 1) // chunk

    def prep(t, carry):
        rows = tile_rows(t)
        q = q_ref[0, rows, :].astype(F32)
        vt = i_ref[0, rows, :].astype(F32).T.astype(BF16)
        ke = []
        for d in range(2):
            z = z_refs[d][0, rows, :].astype(F32)
            lb = lbs[d]
            f = lb + (1.0 - lb) * jax.nn.sigmoid(z)
            k = 1.0 - f
            lf = jnp.log(f)
            hi = lf.astype(BF16)
            lo = (lf - hi.astype(F32)).astype(BF16)
            x2 = jnp.concatenate([hi, lo], axis=1)
            incl, per_chunk = mats[d]
            r = jnp.dot(incl, x2, preferred_element_type=F32)
            bb = r[:, :dk] + r[:, dk:]
            r = jnp.dot(per_chunk, x2, preferred_element_type=F32)
            r = r[:, :dk] + r[:, dk:]
            tot = r[0:cpt]
            a1 = bb - _rows_to_chunks(r[cpt:2 * cpt], chunk)
            a3 = _rows_to_chunks(tot, chunk) - bb
            lanes = slice(d * dk, (d + 1) * dk)
            qa_s[d, rows, :] = (q * jnp.exp(a1)).astype(BF16)
            ka_s[d, rows, :] = (k * jnp.exp(-a1)).astype(BF16)
            qi_s[rows, lanes] = (q * jnp.exp(bb)).astype(BF16)
            ke.append((k * jnp.exp(a3)).astype(BF16))
            dec_s[d, pl.ds(pl.multiple_of(t * cpt, cpt), cpt), :] = jnp.exp(tot)
        zero = jnp.zeros_like(vt)
        lhs = jnp.concatenate([jnp.where(col_chunk == j, vt, zero) for j in range(cpt)], axis=0)
        u = jnp.dot(lhs, jnp.concatenate(ke, axis=1), preferred_element_type=F32)
        for j in range(cpt):
            u_s[:, chunk_cols(t * cpt + j)] = u[j * dk:(j + 1) * dk]
        return carry

    lax.fori_loop(0, n_tiles, prep, 0, unroll=4)

    def scan(n, carry):
        s_f, s_b = carry
        nb = n_chunks - 1 - n
        st_s[:, state_cols(n, 0)] = s_f.astype(BF16)
        s_f = dec_s[0, pl.ds(n, 1), :] * s_f + u_s[:, state_cols(n, 0)]
        st_s[:, state_cols(nb, 1)] = s_b.astype(BF16)
        s_b = dec_s[1, pl.ds(nb, 1), :] * s_b + u_s[:, state_cols(nb, 1)]
        return s_f, s_b

    zero_state = jnp.zeros((dk, dk), F32)
    lax.fori_loop(0, n_chunks, scan, (zero_state, zero_state))

    tt = lax.broadcasted_iota(jnp.int32, (tile, tile), 0)
    uu = lax.broadcasted_iota(jnp.int32, (tile, tile), 1)
    same = (tt // chunk) == (uu // chunk)
    causal = (same & (uu <= tt), same & (uu >= tt))
    ng = ng_ref[0]

    def emit(t, carry):
        rows = tile_rows(t)
        v = i_ref[0, rows, :]
        pm = []
        for d in range(2):
            sc = lax.dot_general(qa_s[d, rows, :], ka_s[d, rows, :], NT_DIMS, preferred_element_type=F32)
            pm.append(jnp.where(causal[d], sc, 0.0).astype(BF16))
        o = jnp.dot(jnp.concatenate(pm, axis=1), jnp.concatenate([v, v], axis=0),
                    preferred_element_type=F32)
        qi = qi_s[rows, :]
        parts = []
        for j in range(cpt):
            parts.append(lax.dot_general(qi[j * chunk:(j + 1) * chunk], st_s[:, chunk_cols(t * cpt + j)],
                                         NT_DIMS, preferred_element_type=F32))
        o = o + jnp.concatenate(parts, axis=0)
        on = o * lax.rsqrt(jnp.mean(o * o, axis=-1, keepdims=True) + RMS_EPS) * ng
        g = g_ref[0, rows, :].astype(F32)
        o_ref[0, rows, :] = (on * (g * jax.nn.sigmoid(g))).astype(o_ref.dtype)
        return carry

    lax.fori_loop(0, n_tiles, emit, 0, unroll=4)


def _hgrn_branch(proj, lb_logits, norm_g, layer, width):
    bsz, seq, _ = proj.shape
    dk = HG_HEAD_DIM
    heads = width // dk
    depth = lb_logits.shape[1]
    n_chunks = seq // HG_CHUNK

    def col(group):
        return pl.BlockSpec((1, seq, dk), lambda b, h: (b, 0, group * heads + h))

    return pl.pallas_call(
        functools.partial(_hgrn_kernel, layer=layer),
        grid=(bsz, heads),
        in_specs=[col(0), col(1), col(2), col(3), col(4),
                  pl.BlockSpec((2, depth, dk), lambda b, h: (0, 0, h)),
                  pl.BlockSpec((1, 1, dk), lambda b, h: (layer, 0, 0))],
        out_specs=pl.BlockSpec((1, seq, dk), lambda b, h: (b, 0, h)),
        out_shape=jax.ShapeDtypeStruct((bsz, seq, width), BF16),
        scratch_shapes=[pltpu.VMEM((2, seq, dk), BF16),
                        pltpu.VMEM((2, seq, dk), BF16),
                        pltpu.VMEM((seq, 2 * dk), BF16),
                        pltpu.VMEM((dk, n_chunks * 2 * dk), F32),
                        pltpu.VMEM((dk, n_chunks * 2 * dk), BF16),
                        pltpu.VMEM((2, n_chunks, dk), F32)],
        compiler_params=_params(2),
        name="hgrn2_branch",
    )(proj, proj, proj, proj, proj, lb_logits, norm_g.reshape(norm_g.shape[0], 1, dk))


def _na_bias_table(rpb_l):
    n_off = 2 * NA_KC - 1
    heads, n_rel = rpb_l.shape[0], rpb_l.shape[1]
    qc = np.arange(GRID_W)
    cs = np.clip(qc - NA_KC // 2, 0, GRID_W - NA_KC)
    kc = np.arange(GRID_W)
    valid = (kc[:, None] >= cs[None, :]) & (kc[:, None] < cs[None, :] + NA_KC)
    col_off = kc[:, None] - qc[None, :] + NA_KC - 1
    onehot = (col_off[None] == np.arange(n_off)[:, None, None]) & valid[None]
    sel = np.zeros((2, n_off, GRID_W, 2, GRID_W), np.float32)
    for h in range(2):
        sel[h, :, :, h, :] = onehot
    sel = sel.reshape(2 * n_off, GRID_W, 2 * GRID_W)
    valid2 = np.concatenate([valid, valid], axis=1)
    rpb_p = rpb_l.astype(F32).reshape(heads // 2, 2, n_rel, n_off).transpose(0, 2, 1, 3)
    rpb_p = rpb_p.reshape(heads // 2, n_rel, 2 * n_off)
    tz = jnp.einsum('prc,ckn->prkn', rpb_p, jnp.asarray(sel), precision=lax.Precision.HIGHEST)
    tz = jnp.where(jnp.asarray(valid2)[None, None], tz, MASK_VALUE)
    return tz.reshape(heads // 2, n_rel * GRID_W, 2 * GRID_W)


def _na_kernel(q_ref, k_ref, v_ref, g_ref, bias_ref, o_ref, vt_s, *, rows):
    w = GRID_W
    seq = rows * w
    kr_n = min(NA_ROWS_MAX, rows)
    half = kr_n // 2
    win = kr_n * w
    tile = MXU_DIM
    scale = NA_HEAD_DIM ** -0.5
    head0 = lax.broadcasted_iota(jnp.int32, (w, LANES), 1) < NA_HEAD_DIM

    for t in range(seq // tile):
        src0 = t * tile
        vt_s[0, :, src0:src0 + tile] = v_ref[0, src0:src0 + tile, :].astype(F32).T.astype(BF16)
        src1 = min(src0 + w, seq - tile)
        vt_s[1, :, src1 - w:src1 - w + tile] = v_ref[0, src1:src1 + tile, :].astype(F32).T.astype(BF16)

    def body(r, carry):
        rs = jnp.clip(r - half, 0, rows - kr_n)
        rel0 = rs - r + (NA_ROWS_MAX - 1)
        brows = pl.ds(pl.multiple_of(rel0 * w, w), win)
        qrows = pl.ds(pl.multiple_of(r * w, w), w)
        krows = pl.ds(pl.multiple_of(rs * w, w), win)
        par = rs % 2
        vcols = pl.ds(pl.multiple_of((rs - par) * w, 2 * w), win)
        q = q_ref[0, qrows, :] * scale
        zq = jnp.zeros_like(q)
        qbd = jnp.concatenate([jnp.where(head0, q, zq), jnp.where(head0, zq, q)], axis=0)
        st = lax.dot_general(k_ref[0, krows, :], qbd, NT_DIMS, preferred_element_type=F32)
        st = st + bias_ref[0, brows, :]
        m = jnp.max(st, axis=0, keepdims=True)
        pt = jnp.exp(st - m)
        l = jnp.sum(pt, axis=0, keepdims=True)
        ot = jnp.dot(vt_s[par, :, vcols], pt.astype(BF16), preferred_element_type=F32) / l
        o2 = ot.T
        o = jnp.where(head0, o2[0:w], o2[w:2 * w])
        g = g_ref[0, qrows, :].astype(F32)
        o_ref[0, qrows, :] = (o * (g * jax.nn.sigmoid(g))).astype(o_ref.dtype)
        return carry

    lax.fori_loop(0, rows, body, 0, unroll=8)


def _na_branch(proj, bias, col0, width):
    bsz, seq, _ = proj.shape
    rows = seq // GRID_W
    pairs = width // LANES

    def col(group):
        return pl.BlockSpec((1, seq, LANES), lambda p, b: (b, 0, col0 + group * pairs + p))

    return pl.pallas_call(
        functools.partial(_na_kernel, rows=rows),
        grid=(pairs, bsz),
        in_specs=[col(0), col(1), col(2), col(3),
                  pl.BlockSpec((1,) + bias.shape[1:], lambda p, b: (p, 0, 0))],
        out_specs=pl.BlockSpec((1, seq, LANES), lambda p, b: (b, 0, p)),
        out_shape=jax.ShapeDtypeStruct((bsz, seq, width), BF16),
        scratch_shapes=[pltpu.VMEM((2, LANES, seq), BF16)],
        compiler_params=_params(2),
        name="neighbourhood_attention",
    )(proj, proj, proj, proj, bias)


def _outproj_kernel(yh_ref, yn_ref, x_ref, mod_ref, w_ref, lg_ref, lb_ref, o_ref, *, d, alpha):
    hw = yh_ref.shape[1]
    y = jnp.dot(yh_ref[...], w_ref[0, pl.ds(0, hw), :], preferred_element_type=F32)
    y = y + jnp.dot(yn_ref[...], w_ref[0, pl.ds(hw, yn_ref.shape[1]), :], preferred_element_type=F32)
    gate = mod_ref[0, 0][:, 2 * d:3 * d]
    z = alpha * x_ref[...] + gate * y
    mu = jnp.mean(z, axis=-1, keepdims=True)
    zc = z - mu
    var = jnp.mean(zc * zc, axis=-1, keepdims=True)
    o_ref[...] = zc * lax.rsqrt(var + LN_EPS) * lg_ref[0] + lb_ref[0]


def _out_projection(y_hg2, y_na2, x2, mod4, w_out_bf, ln_g, ln_b, layer, seq, tm, alpha):
    m, d = x2.shape
    hw, nw = y_hg2.shape[1], y_na2.shape[1]
    return pl.pallas_call(
        functools.partial(_outproj_kernel, d=d, alpha=alpha),
        grid=(m // tm,),
        in_specs=[pl.BlockSpec((tm, hw), lambda i: (i, 0)),
                  pl.BlockSpec((tm, nw), lambda i: (i, 0)),
                  pl.BlockSpec((tm, d), lambda i: (i, 0)),
                  pl.BlockSpec((1, 1, 1, 3 * d), lambda i: (layer, (i * tm) // seq, 0, 0)),
                  pl.BlockSpec((1, hw + nw, d), lambda i: (layer, 0, 0)),
                  pl.BlockSpec((1, 1, d), lambda i: (layer, 0, 0)),
                  pl.BlockSpec((1, 1, d), lambda i: (layer, 0, 0))],
        out_specs=pl.BlockSpec((tm, d), lambda i: (i, 0)),
        out_shape=jax.ShapeDtypeStruct((m, d), F32),
        compiler_params=_params(1),
        name="out_projection_deepnorm_ln",
    )(y_hg2, y_na2, x2, mod4, w_out_bf, ln_g.reshape(-1, 1, d), ln_b.reshape(-1, 1, d))


def _projection_tiles(m, seq, n_in):
    tm_in = seq
    tn_in = max(t for t in range(MXU_DIM, 4 * MXU_DIM + 1, MXU_DIM) if n_in % t == 0)
    tm_out = seq // 2
    assert m % tm_in == 0 and m % tm_out == 0 and seq % tm_out == 0
    return tm_in, tn_in, tm_out


def kernel(x, c, ada_w, ada_b, w_in, lb_logits, hg_norm_g, rpb, w_out, ln_g, ln_b):
    bsz, seq, d = x.shape
    depth = w_in.shape[0]
    mix = w_out.shape[1]
    hg_width = mix // 2
    na_width = mix - hg_width
    rows = seq // GRID_W
    alpha = (2 * depth) ** 0.25
    tm_in, tn_in, tm_out = _projection_tiles(bsz * seq, seq, w_in.shape[2])

    mod = _ada_modulation(c, ada_w, ada_b)
    mod4 = mod.reshape(depth, bsz, 1, 3 * d)
    w_in_bf = w_in.astype(BF16)
    w_out_bf = w_out.astype(BF16)
    na_col0 = 5 * hg_width // LANES

    x2 = x.reshape(bsz * seq, d)
    for layer in range(depth):
        proj = _in_projection(x2, mod4, w_in_bf, layer, seq, tm_in, tn_in).reshape(bsz, seq, -1)
        y_hg = _hgrn_branch(proj, lb_logits, hg_norm_g, layer, hg_width)
        y_na = _na_branch(proj, _na_bias_table(rpb[layer]), na_col0, na_width)
        x2 = _out_projection(y_hg.reshape(bsz * seq, hg_width), y_na.reshape(bsz * seq, na_width),
                             x2, mod4, w_out_bf, ln_g, ln_b, layer, seq, tm_out, alpha)
    return x2.reshape(bsz, seq, d)
```

```python
import functools

import numpy as np
import jax
import jax.numpy as jnp
from jax import lax
from jax.experimental import pallas as pl
from jax.experimental.pallas import tpu as pltpu

GRID_W = 64
HG_HEAD_DIM = 128
HG_CHUNK = 32
NA_HEAD_DIM = 64
NA_ROWS_MAX = 8
NA_KC = 16
LN_EPS = 1e-5
RMS_EPS = 1e-6
MASK_VALUE = -1e30

LANES = 128
MXU_DIM = 256
VMEM_LIMIT_BYTES = 56 * 1024 * 1024

F32 = jnp.float32
BF16 = jnp.bfloat16
NT_DIMS = (((1,), (1,)), ((), ()))


def _params(n_axes):
    return pltpu.CompilerParams(dimension_semantics=("arbitrary",) * n_axes,
                                vmem_limit_bytes=VMEM_LIMIT_BYTES)


def _ada_kernel(c_ref, w_ref, b_ref, o_ref):
    c = c_ref[...]
    a = c * jax.nn.sigmoid(c)
    o_ref[0] = jnp.dot(a, w_ref[0], precision=lax.Precision.HIGHEST,
                       preferred_element_type=F32) + b_ref[0]


def _ada_modulation(c, ada_w, ada_b):
    depth, d, cols = ada_w.shape
    bsz = c.shape[0]
    nj = cols // d
    return pl.pallas_call(
        _ada_kernel,
        grid=(depth, nj),
        in_specs=[pl.BlockSpec((bsz, d), lambda l, j: (0, 0)),
                  pl.BlockSpec((1, d, d), lambda l, j: (l, 0, j)),
                  pl.BlockSpec((1, 1, d), lambda l, j: (l, 0, j))],
        out_specs=pl.BlockSpec((1, bsz, d), lambda l, j: (l, 0, j)),
        out_shape=jax.ShapeDtypeStruct((depth, bsz, cols), F32),
        compiler_params=_params(2),
        name="ada_modulation",
    )(c, ada_w, ada_b.reshape(depth, 1, cols))


def _inproj_kernel(x_ref, mod_ref, w_ref, o_ref, h_ref, *, d):
    @pl.when(pl.program_id(1) == 0)
    def _():
        x = x_ref[...]
        mu = jnp.mean(x, axis=-1, keepdims=True)
        xc = x - mu
        var = jnp.mean(xc * xc, axis=-1, keepdims=True)
        hn = xc * lax.rsqrt(var + LN_EPS)
        m = mod_ref[0, 0]
        h_ref[...] = (hn * (1.0 + m[:, d:2 * d]) + m[:, 0:d]).astype(BF16)

    o_ref[...] = jnp.dot(h_ref[...], w_ref[0], preferred_element_type=F32).astype(o_ref.dtype)


def _in_projection(x2, mod4, w_in_bf, layer, seq, tm, tn):
    m, d = x2.shape
    n = w_in_bf.shape[2]
    return pl.pallas_call(
        functools.partial(_inproj_kernel, d=d),
        grid=(m // tm, n // tn),
        in_specs=[pl.BlockSpec((tm, d), lambda i, j: (i, 0)),
                  pl.BlockSpec((1, 1, 1, 3 * d), lambda i, j: (layer, (i * tm) // seq, 0, 0)),
                  pl.BlockSpec((1, d, tn), lambda i, j: (layer, 0, j))],
        out_specs=pl.BlockSpec((tm, tn), lambda i, j: (i, j)),
        out_shape=jax.ShapeDtypeStruct((m, n), BF16),
        scratch_shapes=[pltpu.VMEM((tm, d), BF16)],
        compiler_params=_params(2),
        name="ln_in_projection",
    )(x2, mod4, w_in_bf)


def _chunk_sum_matrices(tile, chunk):
    t = lax.broadcasted_iota(jnp.int32, (tile, tile), 0)
    u = lax.broadcasted_iota(jnp.int32, (tile, tile), 1)
    same = (t // chunk) == (u // chunk)
    n = tile // chunk
    cj = lax.broadcasted_iota(jnp.int32, (2 * n, tile), 0)
    cu = lax.broadcasted_iota(jnp.int32, (2 * n, tile), 1)
    is_total = cj < n
    in_chunk = (cu // chunk) == jnp.where(is_total, cj, cj - n)
    pos = cu % chunk

    def as_bf16(m):
        return jnp.where(m, 1.0, 0.0).astype(BF16)

    fw = (as_bf16(same & (u <= t)), as_bf16(in_chunk & (is_total | (pos <= chunk // 2 - 1))))
    bw = (as_bf16(same & (u >= t)), as_bf16(in_chunk & (is_total | (pos >= chunk // 2))))
    return fw, bw


def _rows_to_chunks(x, chunk):
    return jnp.concatenate([jnp.broadcast_to(x[j:j + 1], (chunk, x.shape[1])) for j in range(x.shape[0])],
                           axis=0)


def _hgrn_kernel(q_ref, i_ref, zf_ref, zb_ref, g_ref, lbl_ref, ng_ref, o_ref,
                 qa_s, ka_s, qi_s, u_s, st_s, dec_s, *, layer):
    seq = q_ref.shape[1]
    dk = q_ref.shape[2]
    tile = MXU_DIM
    chunk = HG_CHUNK
    cpt = tile // chunk
    n_tiles = seq // tile
    n_chunks = seq // chunk

    lg = lbl_ref[...]
    e = jnp.exp(lg - jnp.max(lg, axis=1, keepdims=True))
    p = e / jnp.sum(e, axis=1, keepdims=True)
    if layer == 0:
        lbs = [jnp.zeros((1, dk), F32)] * 2
    else:
        lbs = [jnp.sum(p[d, 1:layer + 1, :], axis=0, keepdims=True) for d in range(2)]

    mats = _chunk_sum_matrices(tile, chunk)
    z_refs = (zf_ref, zb_ref)

    def tile_rows(t):
        return pl.ds(pl.multiple_of(t * tile, tile), tile)

    def chunk_cols(n):
        return pl.ds(pl.multiple_of(n * 2 * dk, 2 * dk), 2 * dk)

    def state_cols(n, d):
        return pl.ds(pl.multiple_of(n * 2 * dk + d * dk, dk), dk)

    col_chunk = lax.broadcasted_iota(jnp.int32, (dk, tile), 1) // chunk

    def prep(t, carry):
        rows = tile_rows(t)
        q = q_ref[0, rows, :].astype(F32)
        vt = i_ref[0, rows, :].astype(F32).T.astype(BF16)
        ke = []
        for d in range(2):
            z = z_refs[d][0, rows, :].astype(F32)
            lb = lbs[d]
            f = lb + (1.0 - lb) * jax.nn.sigmoid(z)
            k = 1.0 - f
            lf = jnp.log(f)
            hi = lf.astype(BF16)
            lo = (lf - hi.astype(F32)).astype(BF16)
            x2 = jnp.concatenate([hi, lo], axis=1)
            incl, per_chunk = mats[d]
            r = jnp.dot(incl, x2, preferred_element_type=F32)
            bb = r[:, :dk] + r[:, dk:]
            r = jnp.dot(per_chunk, x2, preferred_element_type=F32)
            r = r[:, :dk] + r[:, dk:]
            tot = r[0:cpt]
            a1 = bb - _rows_to_chunks(r[cpt:2 * cpt], chunk)
            a3 = _rows_to_chunks(tot, chunk) - bb
            lanes = slice(d * dk, (d + 1) * dk)
            qa_s[d, rows, :] = (q * jnp.exp(a1)).astype(BF16)
            ka_s[d, rows, :] = (k * jnp.exp(-a1)).astype(BF16)
            qi_s[rows, lanes] = (q * jnp.exp(bb)).astype(BF16)
            ke.append((k * jnp.exp(a3)).astype(BF16))
            dec_s[d, pl.ds(pl.multiple_of(t * cpt, cpt), cpt), :] = jnp.exp(tot)
        zero = jnp.zeros_like(vt)
        lhs = jnp.concatenate([jnp.where(col_chunk == j, vt, zero) for j in range(cpt)], axis=0)
        u = jnp.dot(lhs, jnp.concatenate(ke, axis=1), preferred_element_type=F32)
        for j in range(cpt):
            u_s[:, chunk_cols(t * cpt + j)] = u[j * dk:(j + 1) * dk]
        return carry

    lax.fori_loop(0, n_tiles, prep, 0, unroll=True)

    def scan(d, i, s):
        n = i if d == 0 else n_chunks - 1 - i
        st_s[:, state_cols(n, d)] = s.astype(BF16)
        return dec_s[d, pl.ds(n, 1), :] * s + u_s[:, state_cols(n, d)]

    for d in range(2):
        lax.fori_loop(0, n_chunks, functools.partial(scan, d), jnp.zeros((dk, dk), F32), unroll=4)

    tt = lax.broadcasted_iota(jnp.int32, (tile, tile), 0)
    uu = lax.broadcasted_iota(jnp.int32, (tile, tile), 1)
    same = (tt // chunk) == (uu // chunk)
    causal = (same & (uu <= tt), same & (uu >= tt))
    ng = ng_ref[0]

    def emit(t, carry):
        rows = tile_rows(t)
        v = i_ref[0, rows, :]
        pm = []
        for d in range(2):
            sc = lax.dot_general(qa_s[d, rows, :], ka_s[d, rows, :], NT_DIMS, preferred_element_type=F32)
            pm.append(jnp.where(causal[d], sc, 0.0).astype(BF16))
        o = jnp.dot(jnp.concatenate(pm, axis=1), jnp.concatenate([v, v], axis=0),
                    preferred_element_type=F32)
        qi = qi_s[rows, :]
        parts = []
        for j in range(cpt):
            parts.append(lax.dot_general(qi[j * chunk:(j + 1) * chunk], st_s[:, chunk_cols(t * cpt + j)],
                                         NT_DIMS, preferred_element_type=F32))
        o = o + jnp.concatenate(parts, axis=0)
        on = o * lax.rsqrt(jnp.mean(o * o, axis=-1, keepdims=True) + RMS_EPS) * ng
        g = g_ref[0, rows, :].astype(F32)
        o_ref[0, rows, :] = (on * (g * jax.nn.sigmoid(g))).astype(o_ref.dtype)
        return carry

    lax.fori_loop(0, n_tiles, emit, 0, unroll=True)


def _hgrn_branch(proj, lb_logits, norm_g, layer, width):
    bsz, seq, _ = proj.shape
    dk = HG_HEAD_DIM
    heads = width // dk
    depth = lb_logits.shape[1]
    n_chunks = seq // HG_CHUNK

    def col(group):
        return pl.BlockSpec((1, seq, dk), lambda b, h: (b, 0, group * heads + h))

    return pl.pallas_call(
        functools.partial(_hgrn_kernel, layer=layer),
        grid=(bsz, heads),
        in_specs=[col(0), col(1), col(2), col(3), col(4),
                  pl.BlockSpec((2, depth, dk), lambda b, h: (0, 0, h)),
                  pl.BlockSpec((1, 1, dk), lambda b, h: (layer, 0, 0))],
        out_specs=pl.BlockSpec((1, seq, dk), lambda b, h: (b, 0, h)),
        out_shape=jax.ShapeDtypeStruct((bsz, seq, width), BF16),
        scratch_shapes=[pltpu.VMEM((2, seq, dk), BF16),
                        pltpu.VMEM((2, seq, dk), BF16),
                        pltpu.VMEM((seq, 2 * dk), BF16),
                        pltpu.VMEM((dk, n_chunks * 2 * dk), F32),
                        pltpu.VMEM((dk, n_chunks * 2 * dk), BF16),
                        pltpu.VMEM((2, n_chunks, dk), F32)],
        compiler_params=_params(2),
        name="hgrn2_branch",
    )(proj, proj, proj, proj, proj, lb_logits, norm_g.reshape(norm_g.shape[0], 1, dk))


def _na_bias_table(rpb_l):
    n_off = 2 * NA_KC - 1
    heads, n_rel = rpb_l.shape[0], rpb_l.shape[1]
    qc = np.arange(GRID_W)
    cs = np.clip(qc - NA_KC // 2, 0, GRID_W - NA_KC)
    kc = np.arange(GRID_W)
    valid = (kc[:, None] >= cs[None, :]) & (kc[:, None] < cs[None, :] + NA_KC)
    col_off = kc[:, None] - qc[None, :] + NA_KC - 1
    onehot = (col_off[None] == np.arange(n_off)[:, None, None]) & valid[None]
    sel = np.zeros((2, n_off, GRID_W, 2, GRID_W), np.float32)
    for h in range(2):
        sel[h, :, :, h, :] = onehot
    sel = sel.reshape(2 * n_off, GRID_W, 2 * GRID_W)
    valid2 = np.concatenate([valid, valid], axis=1)
    rpb_p = rpb_l.astype(F32).reshape(heads // 2, 2, n_rel, n_off).transpose(0, 2, 1, 3)
    rpb_p = rpb_p.reshape(heads // 2, n_rel, 2 * n_off)
    tz = jnp.einsum('prc,ckn->prkn', rpb_p, jnp.asarray(sel), precision=lax.Precision.HIGHEST)
    tz = jnp.where(jnp.asarray(valid2)[None, None], tz, MASK_VALUE)
    return tz.reshape(heads // 2, n_rel * GRID_W, 2 * GRID_W)


def _na_kernel(q_ref, k_ref, v_ref, g_ref, bias_ref, o_ref, vt_s, *, rows):
    w = GRID_W
    seq = rows * w
    kr_n = min(NA_ROWS_MAX, rows)
    half = kr_n // 2
    win = kr_n * w
    tile = MXU_DIM
    scale = NA_HEAD_DIM ** -0.5
    head0 = lax.broadcasted_iota(jnp.int32, (w, LANES), 1) < NA_HEAD_DIM

    for t in range(seq // tile):
        src0 = t * tile
        vt_s[0, :, src0:src0 + tile] = v_ref[0, src0:src0 + tile, :].astype(F32).T.astype(BF16)
        src1 = min(src0 + w, seq - tile)
        vt_s[1, :, src1 - w:src1 - w + tile] = v_ref[0, src1:src1 + tile, :].astype(F32).T.astype(BF16)

    for r in range(rows):
        rs = min(max(r - half, 0), rows - kr_n)
        rel0 = rs - r + (NA_ROWS_MAX - 1)
        brows = pl.ds(rel0 * w, win)
        qrows = pl.ds(r * w, w)
        krows = pl.ds(rs * w, win)
        par = rs % 2
        vcols = pl.ds((rs - par) * w, win)
        q = q_ref[0, qrows, :] * scale
        zq = jnp.zeros_like(q)
        qbd = jnp.concatenate([jnp.where(head0, q, zq), jnp.where(head0, zq, q)], axis=0)
        st = lax.dot_general(k_ref[0, krows, :], qbd, NT_DIMS, preferred_element_type=F32)
        st = st + bias_ref[0, brows, :]
        m = jnp.max(st, axis=0, keepdims=True)
        pt = jnp.exp(st - m)
        l = jnp.sum(pt, axis=0, keepdims=True)
        ot = jnp.dot(vt_s[par, :, vcols], pt.astype(BF16), preferred_element_type=F32) / l
        o2 = ot.T
        o = jnp.where(head0, o2[0:w], o2[w:2 * w])
        g = g_ref[0, qrows, :].astype(F32)
        o_ref[0, qrows, :] = (o * (g * jax.nn.sigmoid(g))).astype(o_ref.dtype)


def _na_branch(proj, bias, col0, width):
    bsz, seq, _ = proj.shape
    rows = seq // GRID_W
    pairs = width // LANES

    def col(group):
        return pl.BlockSpec((1, seq, LANES), lambda p, b: (b, 0, col0 + group * pairs + p))

    return pl.pallas_call(
        functools.partial(_na_kernel, rows=rows),
        grid=(pairs, bsz),
        in_specs=[col(0), col(1), col(2), col(3),
                  pl.BlockSpec((1,) + bias.shape[1:], lambda p, b: (p, 0, 0))],
        out_specs=pl.BlockSpec((1, seq, LANES), lambda p, b: (b, 0, p)),
        out_shape=jax.ShapeDtypeStruct((bsz, seq, width), BF16),
        scratch_shapes=[pltpu.VMEM((2, LANES, seq), BF16)],
        compiler_params=_params(2),
        name="neighbourhood_attention",
    )(proj, proj, proj, proj, bias)


def _outproj_kernel(yh_ref, yn_ref, x_ref, mod_ref, w_ref, lg_ref, lb_ref, o_ref, *, d, alpha):
    hw = yh_ref.shape[1]
    y = jnp.dot(yh_ref[...], w_ref[0, pl.ds(0, hw), :], preferred_element_type=F32)
    y = y + jnp.dot(yn_ref[...], w_ref[0, pl.ds(hw, yn_ref.shape[1]), :], preferred_element_type=F32)
    gate = mod_ref[0, 0][:, 2 * d:3 * d]
    z = alpha * x_ref[...] + gate * y
    mu = jnp.mean(z, axis=-1, keepdims=True)
    zc = z - mu
    var = jnp.mean(zc * zc, axis=-1, keepdims=True)
    o_ref[...] = zc * lax.rsqrt(var + LN_EPS) * lg_ref[0] + lb_ref[0]


def _out_projection(y_hg2, y_na2, x2, mod4, w_out_bf, ln_g, ln_b, layer, seq, tm, alpha):
    m, d = x2.shape
    hw, nw = y_hg2.shape[1], y_na2.shape[1]
    return pl.pallas_call(
        functools.partial(_outproj_kernel, d=d, alpha=alpha),
        grid=(m // tm,),
        in_specs=[pl.BlockSpec((tm, hw), lambda i: (i, 0)),
                  pl.BlockSpec((tm, nw), lambda i: (i, 0)),
                  pl.BlockSpec((tm, d), lambda i: (i, 0)),
                  pl.BlockSpec((1, 1, 1, 3 * d), lambda i: (layer, (i * tm) // seq, 0, 0)),
                  pl.BlockSpec((1, hw + nw, d), lambda i: (layer, 0, 0)),
                  pl.BlockSpec((1, 1, d), lambda i: (layer, 0, 0)),
                  pl.BlockSpec((1, 1, d), lambda i: (layer, 0, 0))],
        out_specs=pl.BlockSpec((tm, d), lambda i: (i, 0)),
        out_shape=jax.ShapeDtypeStruct((m, d), F32),
        compiler_params=_params(1),
        name="out_projection_deepnorm_ln",
    )(y_hg2, y_na2, x2, mod4, w_out_bf, ln_g.reshape(-1, 1, d), ln_b.reshape(-1, 1, d))


def _projection_tiles(m, seq, n_in):
    tm_in = seq
    tn_in = max(t for t in range(MXU_DIM, 4 * MXU_DIM + 1, MXU_DIM) if n_in % t == 0)
    tm_out = seq // 2
    assert m % tm_in == 0 and m % tm_out == 0 and seq % tm_out == 0
    return tm_in, tn_in, tm_out


def kernel(x, c, ada_w, ada_b, w_in, lb_logits, hg_norm_g, rpb, w_out, ln_g, ln_b):
    bsz, seq, d = x.shape
    depth = w_in.shape[0]
    mix = w_out.shape[1]
    hg_width = mix // 2
    na_width = mix - hg_width
    rows = seq // GRID_W
    alpha = (2 * depth) ** 0.25
    tm_in, tn_in, tm_out = _projection_tiles(bsz * seq, seq, w_in.shape[2])

    mod = _ada_modulation(c, ada_w, ada_b)
    mod4 = mod.reshape(depth, bsz, 1, 3 * d)
    w_in_bf = w_in.astype(BF16)
    w_out_bf = w_out.astype(BF16)
    na_col0 = 5 * hg_width // LANES

    x2 = x.reshape(bsz * seq, d)
    for layer in range(depth):
        proj = _in_projection(x2, mod4, w_in_bf, layer, seq, tm_in, tn_in).reshape(bsz, seq, -1)
        y_hg = _hgrn_branch(proj, lb_logits, hg_norm_g, layer, hg_width)
        y_na = _na_branch(proj, _na_bias_table(rpb[layer]), na_col0, na_width)
        x2 = _out_projection(y_hg.reshape(bsz * seq, hg_width), y_na.reshape(bsz * seq, na_width),
                             x2, mod4, w_out_bf, ln_g, ln_b, layer, seq, tm_out, alpha)
    return x2.reshape(bsz, seq, d)
```

```python
import functools

import numpy as np
import jax
import jax.numpy as jnp
from jax import lax
from jax.experimental import pallas as pl
from jax.experimental.pallas import tpu as pltpu

GRID_W = 64
HG_HEAD_DIM = 128
HG_CHUNK = 32
NA_HEAD_DIM = 64
NA_ROWS_MAX = 8
NA_KC = 16
LN_EPS = 1e-5
RMS_EPS = 1e-6
MASK_VALUE = -1e30

LANES = 128
MXU_DIM = 256
VMEM_LIMIT_BYTES = 56 * 1024 * 1024

F32 = jnp.float32
BF16 = jnp.bfloat16
NT_DIMS = (((1,), (1,)), ((), ()))


def _params(n_axes):
    return pltpu.CompilerParams(dimension_semantics=("arbitrary",) * n_axes,
                                vmem_limit_bytes=VMEM_LIMIT_BYTES)


def _ada_kernel(c_ref, w_ref, b_ref, o_ref):
    c = c_ref[...]
    a = c * jax.nn.sigmoid(c)
    o_ref[0] = jnp.dot(a, w_ref[0], precision=lax.Precision.HIGHEST,
                       preferred_element_type=F32) + b_ref[0]


def _ada_modulation(c, ada_w, ada_b):
    depth, d, cols = ada_w.shape
    bsz = c.shape[0]
    nj = cols // d
    return pl.pallas_call(
        _ada_kernel,
        grid=(depth, nj),
        in_specs=[pl.BlockSpec((bsz, d), lambda l, j: (0, 0)),
                  pl.BlockSpec((1, d, d), lambda l, j: (l, 0, j)),
                  pl.BlockSpec((1, 1, d), lambda l, j: (l, 0, j))],
        out_specs=pl.BlockSpec((1, bsz, d), lambda l, j: (l, 0, j)),
        out_shape=jax.ShapeDtypeStruct((depth, bsz, cols), F32),
        compiler_params=_params(2),
        name="ada_modulation",
    )(c, ada_w, ada_b.reshape(depth, 1, cols))


def _inproj_kernel(x_ref, mod_ref, w_ref, o_ref, h_ref, *, d):
    @pl.when(pl.program_id(1) == 0)
    def _():
        x = x_ref[...]
        mu = jnp.mean(x, axis=-1, keepdims=True)
        xc = x - mu
        var = jnp.mean(xc * xc, axis=-1, keepdims=True)
        hn = xc * lax.rsqrt(var + LN_EPS)
        m = mod_ref[0, 0]
        h_ref[...] = (hn * (1.0 + m[:, d:2 * d]) + m[:, 0:d]).astype(BF16)

    o_ref[...] = jnp.dot(h_ref[...], w_ref[0].astype(BF16), preferred_element_type=F32).astype(o_ref.dtype)


def _in_projection(x2, mod4, w_in, layer, seq, tm, tn):
    m, d = x2.shape
    n = w_in.shape[2]
    return pl.pallas_call(
        functools.partial(_inproj_kernel, d=d),
        grid=(m // tm, n // tn),
        in_specs=[pl.BlockSpec((tm, d), lambda i, j: (i, 0)),
                  pl.BlockSpec((1, 1, 1, 3 * d), lambda i, j: (layer, (i * tm) // seq, 0, 0)),
                  pl.BlockSpec((1, d, tn), lambda i, j: (layer, 0, j))],
        out_specs=pl.BlockSpec((tm, tn), lambda i, j: (i, j)),
        out_shape=jax.ShapeDtypeStruct((m, n), BF16),
        scratch_shapes=[pltpu.VMEM((tm, d), BF16)],
        compiler_params=_params(2),
        name="ln_in_projection",
    )(x2, mod4, w_in)


def _chunk_sum_matrices(tile, chunk):
    t = lax.broadcasted_iota(jnp.int32, (tile, tile), 0)
    u = lax.broadcasted_iota(jnp.int32, (tile, tile), 1)
    same = (t // chunk) == (u // chunk)
    n = tile // chunk
    cj = lax.broadcasted_iota(jnp.int32, (2 * n, tile), 0)
    cu = lax.broadcasted_iota(jnp.int32, (2 * n, tile), 1)
    is_total = cj < n
    in_chunk = (cu // chunk) == jnp.where(is_total, cj, cj - n)
    pos = cu % chunk

    def as_bf16(m):
        return jnp.where(m, 1.0, 0.0).astype(BF16)

    fw = (as_bf16(same & (u <= t)), as_bf16(in_chunk & (is_total | (pos <= chunk // 2 - 1))))
    bw = (as_bf16(same & (u >= t)), as_bf16(in_chunk & (is_total | (pos >= chunk // 2))))
    return fw, bw


def _rows_to_chunks(x, chunk):
    return jnp.concatenate([jnp.broadcast_to(x[j:j + 1], (chunk, x.shape[1])) for j in range(x.shape[0])],
                           axis=0)


def _hgrn_kernel(q_ref, i_ref, zf_ref, zb_ref, g_ref, lbl_ref, ng_ref, o_ref,
                 qa_s, ka_s, qi_s, u_s, acc_s, dec_s, *, layer):
    seq = q_ref.shape[1]
    dk = q_ref.shape[2]
    tile = MXU_DIM
    chunk = HG_CHUNK
    cpt = tile // chunk
    n_tiles = seq // tile
    n_chunks = seq // chunk

    lg = lbl_ref[...]
    e = jnp.exp(lg - jnp.max(lg, axis=1, keepdims=True))
    p = e / jnp.sum(e, axis=1, keepdims=True)
    if layer == 0:
        lbs = [jnp.zeros((1, dk), F32)] * 2
    else:
        lbs = [jnp.sum(p[d, 1:layer + 1, :], axis=0, keepdims=True) for d in range(2)]

    mats = _chunk_sum_matrices(tile, chunk)
    z_refs = (zf_ref, zb_ref)

    col_chunk = lax.broadcasted_iota(jnp.int32, (dk, tile), 1) // chunk

    def prep(t, carry):
        rows = slice(t * tile, (t + 1) * tile)
        q = q_ref[0, rows, :].astype(F32)
        vt = i_ref[0, rows, :].astype(F32).T.astype(BF16)
        ke = []
        for d in range(2):
            z = z_refs[d][0, rows, :].astype(F32)
            lb = lbs[d]
            f = lb + (1.0 - lb) * jax.nn.sigmoid(z)
            k = 1.0 - f
            lf = jnp.log(f)
            hi = lf.astype(BF16)
            lo = (lf - hi.astype(F32)).astype(BF16)
            x2 = jnp.concatenate([hi, lo], axis=1)
            incl, per_chunk = mats[d]
            r = jnp.dot(incl, x2, preferred_element_type=F32)
            bb = r[:, :dk] + r[:, dk:]
            r = jnp.dot(per_chunk, x2, preferred_element_type=F32)
            r = r[:, :dk] + r[:, dk:]
            tot = r[0:cpt]
            mid = r[cpt:2 * cpt]
            a1 = bb - _rows_to_chunks(mid, chunk)
            qa = q * jnp.exp(a1)
            ka = k * jnp.exp(-a1)
            lanes = slice(d * dk, (d + 1) * dk)
            qa_s[d, rows, :] = qa.astype(BF16)
            ka_s[d, rows, :] = ka.astype(BF16)
            qi_s[rows, lanes] = (qa * _rows_to_chunks(jnp.exp(mid), chunk)).astype(BF16)
            ke.append((ka * _rows_to_chunks(jnp.exp(tot - mid), chunk)).astype(BF16))
            dec_s[d, t * cpt:(t + 1) * cpt, :] = jnp.exp(tot)
        zero = jnp.zeros_like(vt)
        lhs = jnp.concatenate([jnp.where(col_chunk == j, vt, zero) for j in range(cpt)], axis=0)
        u = jnp.dot(lhs, jnp.concatenate(ke, axis=1), preferred_element_type=F32)
        for j in range(cpt):
            n = t * cpt + j
            u_s[:, n * 2 * dk:(n + 1) * 2 * dk] = u[j * dk:(j + 1) * dk]
        return carry

    for t in range(n_tiles):
        prep(t, 0)

    tt = lax.broadcasted_iota(jnp.int32, (tile, tile), 0)
    uu = lax.broadcasted_iota(jnp.int32, (tile, tile), 1)
    same = (tt // chunk) == (uu // chunk)
    causal = (same & (uu <= tt), same & (uu >= tt))
    ng = ng_ref[0]

    def direction_tile(d, t, s):
        rows = slice(t * tile, (t + 1) * tile)
        sc = lax.dot_general(qa_s[d, rows, :], ka_s[d, rows, :], NT_DIMS, preferred_element_type=F32)
        o = jnp.dot(jnp.where(causal[d], sc, 0.0).astype(BF16), i_ref[0, rows, :], preferred_element_type=F32)
        qi = qi_s[rows, d * dk:(d + 1) * dk]
        parts = [None] * cpt
        for j in (range(cpt) if d == 0 else reversed(range(cpt))):
            n = t * cpt + j
            parts[j] = lax.dot_general(qi[j * chunk:(j + 1) * chunk], s.astype(BF16), NT_DIMS,
                                       preferred_element_type=F32)
            s = dec_s[d, n:n + 1, :] * s + u_s[:, n * 2 * dk + d * dk:n * 2 * dk + (d + 1) * dk]
        return o + jnp.concatenate(parts, axis=0), s

    s = jnp.zeros((dk, dk), F32)
    for t in range(n_tiles):
        o, s = direction_tile(0, t, s)
        acc_s[t * tile:(t + 1) * tile, :] = o

    s = jnp.zeros((dk, dk), F32)
    for t in reversed(range(n_tiles)):
        rows = slice(t * tile, (t + 1) * tile)
        o, s = direction_tile(1, t, s)
        o = o + acc_s[rows, :]
        on = o * lax.rsqrt(jnp.mean(o * o, axis=-1, keepdims=True) + RMS_EPS) * ng
        g = g_ref[0, rows, :].astype(F32)
        o_ref[0, rows, :] = (on * (g * jax.nn.sigmoid(g))).astype(o_ref.dtype)


def _hgrn_branch(proj, lb_logits, norm_g, layer, width):
    bsz, seq, _ = proj.shape
    dk = HG_HEAD_DIM
    heads = width // dk
    depth = lb_logits.shape[1]
    n_chunks = seq // HG_CHUNK

    def col(group):
        return pl.BlockSpec((1, seq, dk), lambda b, h: (b, 0, group * heads + h))

    return pl.pallas_call(
        functools.partial(_hgrn_kernel, layer=layer),
        grid=(bsz, heads),
        in_specs=[col(0), col(1), col(2), col(3), col(4),
                  pl.BlockSpec((2, depth, dk), lambda b, h: (0, 0, h)),
                  pl.BlockSpec((1, 1, dk), lambda b, h: (layer, 0, 0))],
        out_specs=pl.BlockSpec((1, seq, dk), lambda b, h: (b, 0, h)),
        out_shape=jax.ShapeDtypeStruct((bsz, seq, width), BF16),
        scratch_shapes=[pltpu.VMEM((2, seq, dk), BF16),
                        pltpu.VMEM((2, seq, dk), BF16),
                        pltpu.VMEM((seq, 2 * dk), BF16),
                        pltpu.VMEM((dk, n_chunks * 2 * dk), F32),
                        pltpu.VMEM((seq, dk), F32),
                        pltpu.VMEM((2, n_chunks, dk), F32)],
        compiler_params=_params(2),
        name="hgrn2_branch",
    )(proj, proj, proj, proj, proj, lb_logits, norm_g.reshape(norm_g.shape[0], 1, dk))


def _na_bias_table(rpb_l):
    n_off = 2 * NA_KC - 1
    heads, n_rel = rpb_l.shape[0], rpb_l.shape[1]
    qc = np.arange(GRID_W)
    cs = np.clip(qc - NA_KC // 2, 0, GRID_W - NA_KC)
    kc = np.arange(GRID_W)
    valid = (kc[:, None] >= cs[None, :]) & (kc[:, None] < cs[None, :] + NA_KC)
    col_off = kc[:, None] - qc[None, :] + NA_KC - 1
    onehot = (col_off[None] == np.arange(n_off)[:, None, None]) & valid[None]
    sel = np.zeros((2, n_off, GRID_W, 2, GRID_W), np.float32)
    for h in range(2):
        sel[h, :, :, h, :] = onehot
    sel = sel.reshape(2 * n_off, GRID_W, 2 * GRID_W)
    valid2 = np.concatenate([valid, valid], axis=1)
    rpb_p = rpb_l.astype(F32).reshape(heads // 2, 2, n_rel, n_off).transpose(0, 2, 1, 3)
    rpb_p = rpb_p.reshape(heads // 2, n_rel, 2 * n_off)
    tz = jnp.einsum('prc,ckn->prkn', rpb_p, jnp.asarray(sel), precision=lax.Precision.HIGHEST)
    tz = jnp.where(jnp.asarray(valid2)[None, None], tz, MASK_VALUE)
    return tz.reshape(heads // 2, n_rel * GRID_W, 2 * GRID_W)


def _na_kernel(q_ref, k_ref, v_ref, g_ref, bias_ref, o_ref, vt_s, *, rows):
    w = GRID_W
    seq = rows * w
    kr_n = min(NA_ROWS_MAX, rows)
    half = kr_n // 2
    win = kr_n * w
    tile = MXU_DIM
    scale = NA_HEAD_DIM ** -0.5
    head0 = lax.broadcasted_iota(jnp.int32, (w, LANES), 1) < NA_HEAD_DIM

    for t in range(seq // tile):
        src0 = t * tile
        vt_s[0, :, src0:src0 + tile] = v_ref[0, src0:src0 + tile, :].astype(F32).T.astype(BF16)
        src1 = min(src0 + w, seq - tile)
        vt_s[1, :, src1 - w:src1 - w + tile] = v_ref[0, src1:src1 + tile, :].astype(F32).T.astype(BF16)

    for r in range(rows):
        rs = min(max(r - half, 0), rows - kr_n)
        rel0 = rs - r + (NA_ROWS_MAX - 1)
        brows = pl.ds(rel0 * w, win)
        qrows = pl.ds(r * w, w)
        krows = pl.ds(rs * w, win)
        par = rs % 2
        vcols = pl.ds((rs - par) * w, win)
        q = q_ref[0, qrows, :] * scale
        zq = jnp.zeros_like(q)
        qbd = jnp.concatenate([jnp.where(head0, q, zq), jnp.where(head0, zq, q)], axis=0)
        st = lax.dot_general(k_ref[0, krows, :], qbd, NT_DIMS, preferred_element_type=F32)
        st = st + bias_ref[0, brows, :]
        m = jnp.max(st, axis=0, keepdims=True)
        pt = jnp.exp(st - m)
        l = jnp.sum(pt, axis=0, keepdims=True)
        ot = jnp.dot(vt_s[par, :, vcols], pt.astype(BF16), preferred_element_type=F32) / l
        o2 = ot.T
        o = jnp.where(head0, o2[0:w], o2[w:2 * w])
        g = g_ref[0, qrows, :].astype(F32)
        o_ref[0, qrows, :] = (o * (g * jax.nn.sigmoid(g))).astype(o_ref.dtype)


def _na_branch(proj, bias, col0, width):
    bsz, seq, _ = proj.shape
    rows = seq // GRID_W
    pairs = width // LANES

    def col(group):
        return pl.BlockSpec((1, seq, LANES), lambda p, b: (b, 0, col0 + group * pairs + p))

    return pl.pallas_call(
        functools.partial(_na_kernel, rows=rows),
        grid=(pairs, bsz),
        in_specs=[col(0), col(1), col(2), col(3),
                  pl.BlockSpec((1,) + bias.shape[1:], lambda p, b: (p, 0, 0))],
        out_specs=pl.BlockSpec((1, seq, LANES), lambda p, b: (b, 0, p)),
        out_shape=jax.ShapeDtypeStruct((bsz, seq, width), BF16),
        scratch_shapes=[pltpu.VMEM((2, LANES, seq), BF16)],
        compiler_params=_params(2),
        name="neighbourhood_attention",
    )(proj, proj, proj, proj, bias)


def _outproj_kernel(yh_ref, yn_ref, x_ref, mod_ref, w_ref, lg_ref, lb_ref, o_ref, *, d, alpha):
    hw = yh_ref.shape[1]
    y = jnp.dot(yh_ref[...], w_ref[0, pl.ds(0, hw), :], preferred_element_type=F32)
    y = y + jnp.dot(yn_ref[...], w_ref[0, pl.ds(hw, yn_ref.shape[1]), :], preferred_element_type=F32)
    gate = mod_ref[0, 0][:, 2 * d:3 * d]
    z = alpha * x_ref[...] + gate * y
    mu = jnp.mean(z, axis=-1, keepdims=True)
    zc = z - mu
    var = jnp.mean(zc * zc, axis=-1, keepdims=True)
    o_ref[...] = zc * lax.rsqrt(var + LN_EPS) * lg_ref[0] + lb_ref[0]


def _out_projection(y_hg2, y_na2, x2, mod4, w_out_bf, ln_g, ln_b, layer, seq, tm, alpha):
    m, d = x2.shape
    hw, nw = y_hg2.shape[1], y_na2.shape[1]
    return pl.pallas_call(
        functools.partial(_outproj_kernel, d=d, alpha=alpha),
        grid=(m // tm,),
        in_specs=[pl.BlockSpec((tm, hw), lambda i: (i, 0)),
                  pl.BlockSpec((tm, nw), lambda i: (i, 0)),
                  pl.BlockSpec((tm, d), lambda i: (i, 0)),
                  pl.BlockSpec((1, 1, 1, 3 * d), lambda i: (layer, (i * tm) // seq, 0, 0)),
                  pl.BlockSpec((1, hw + nw, d), lambda i: (layer, 0, 0)),
                  pl.BlockSpec((1, 1, d), lambda i: (layer, 0, 0)),
                  pl.BlockSpec((1, 1, d), lambda i: (layer, 0, 0))],
        out_specs=pl.BlockSpec((tm, d), lambda i: (i, 0)),
        out_shape=jax.ShapeDtypeStruct((m, d), F32),
        compiler_params=_params(1),
        name="out_projection_deepnorm_ln",
    )(y_hg2, y_na2, x2, mod4, w_out_bf, ln_g.reshape(-1, 1, d), ln_b.reshape(-1, 1, d))


def _projection_tiles(m, seq, n_in):
    tm_in = seq
    tn_in = max(t for t in range(MXU_DIM, 4 * MXU_DIM + 1, MXU_DIM) if n_in % t == 0)
    tm_out = seq // 2
    assert m % tm_in == 0 and m % tm_out == 0 and seq % tm_out == 0
    return tm_in, tn_in, tm_out


def kernel(x, c, ada_w, ada_b, w_in, lb_logits, hg_norm_g, rpb, w_out, ln_g, ln_b):
    bsz, seq, d = x.shape
    depth = w_in.shape[0]
    mix = w_out.shape[1]
    hg_width = mix // 2
    na_width = mix - hg_width
    rows = seq // GRID_W
    alpha = (2 * depth) ** 0.25
    tm_in, tn_in, tm_out = _projection_tiles(bsz * seq, seq, w_in.shape[2])

    mod = _ada_modulation(c, ada_w, ada_b)
    mod4 = mod.reshape(depth, bsz, 1, 3 * d)
    w_out_bf = w_out.astype(BF16)
    na_col0 = 5 * hg_width // LANES

    x2 = x.reshape(bsz * seq, d)
    for layer in range(depth):
        proj = _in_projection(x2, mod4, w_in, layer, seq, tm_in, tn_in).reshape(bsz, seq, -1)
        y_hg = _hgrn_branch(proj, lb_logits, hg_norm_g, layer, hg_width)
        y_na = _na_branch(proj, _na_bias_table(rpb[layer]), na_col0, na_width)
        x2 = _out_projection(y_hg.reshape(bsz * seq, hg_width), y_na.reshape(bsz * seq, na_width),
                             x2, mod4, w_out_bf, ln_g, ln_b, layer, seq, tm_out, alpha)
    return x2.reshape(bsz, seq, d)
```

```python
import functools

import numpy as np
import jax
import jax.numpy as jnp
from jax import lax
from jax.experimental import pallas as pl
from jax.experimental.pallas import tpu as pltpu

GRID_W = 64
HG_HEAD_DIM = 128
HG_CHUNK = 32
NA_HEAD_DIM = 64
NA_ROWS_MAX = 8
NA_KC = 16
LN_EPS = 1e-5
RMS_EPS = 1e-6
MASK_VALUE = -1e30
OUT_ROW_SLICES = 4
IN_ROW_SLICES = 8

LANES = 128
MXU_DIM = 256
VMEM_LIMIT_BYTES = 56 * 1024 * 1024

F32 = jnp.float32
BF16 = jnp.bfloat16
NT_DIMS = (((1,), (1,)), ((), ()))


def _params(n_axes):
    return pltpu.CompilerParams(dimension_semantics=("arbitrary",) * n_axes,
                                vmem_limit_bytes=VMEM_LIMIT_BYTES)


def _ada_kernel(c_ref, w_ref, b_ref, o_ref):
    c = c_ref[...]
    a = c * jax.nn.sigmoid(c)
    o_ref[0] = jnp.dot(a, w_ref[0], precision=lax.Precision.HIGHEST,
                       preferred_element_type=F32) + b_ref[0]


def _ada_modulation(c, ada_w, ada_b):
    depth, d, cols = ada_w.shape
    bsz = c.shape[0]
    nj = cols // d
    return pl.pallas_call(
        _ada_kernel,
        grid=(depth, nj),
        in_specs=[pl.BlockSpec((bsz, d), lambda l, j: (0, 0)),
                  pl.BlockSpec((1, d, d), lambda l, j: (l, 0, j)),
                  pl.BlockSpec((1, 1, d), lambda l, j: (l, 0, j))],
        out_specs=pl.BlockSpec((1, bsz, d), lambda l, j: (l, 0, j)),
        out_shape=jax.ShapeDtypeStruct((depth, bsz, cols), F32),
        compiler_params=_params(2),
        name="ada_modulation",
    )(c, ada_w, ada_b.reshape(depth, 1, cols))


def _inproj_kernel(x_ref, mod_ref, w_ref, o_ref, h_ref, *, d):
    first = pl.program_id(1) == 0

    @pl.when(first)
    def _():
        w = w_ref[0].astype(BF16)
        m = mod_ref[0, 0]
        sub = x_ref.shape[0] // IN_ROW_SLICES
        for c in range(IN_ROW_SLICES):
            rows = slice(c * sub, (c + 1) * sub)
            x = x_ref[rows, :]
            mu = jnp.mean(x, axis=-1, keepdims=True)
            xc = x - mu
            var = jnp.mean(xc * xc, axis=-1, keepdims=True)
            h = (xc * lax.rsqrt(var + LN_EPS) * (1.0 + m[:, d:2 * d]) + m[:, 0:d]).astype(BF16)
            h_ref[rows, :] = h
            o_ref[rows, :] = jnp.dot(h, w, preferred_element_type=F32).astype(o_ref.dtype)

    @pl.when(jnp.logical_not(first))
    def _():
        o_ref[...] = jnp.dot(h_ref[...], w_ref[0].astype(BF16), preferred_element_type=F32).astype(o_ref.dtype)


def _in_projection(x2, mod4, w_in, layer, seq, tm, tn):
    m, d = x2.shape
    n = w_in.shape[2]
    return pl.pallas_call(
        functools.partial(_inproj_kernel, d=d),
        grid=(m // tm, n // tn),
        in_specs=[pl.BlockSpec((tm, d), lambda i, j: (i, 0)),
                  pl.BlockSpec((1, 1, 1, 3 * d), lambda i, j: (layer, (i * tm) // seq, 0, 0)),
                  pl.BlockSpec((1, d, tn), lambda i, j: (layer, 0, j))],
        out_specs=pl.BlockSpec((tm, tn), lambda i, j: (i, j)),
        out_shape=jax.ShapeDtypeStruct((m, n), BF16),
        scratch_shapes=[pltpu.VMEM((tm, d), BF16)],
        compiler_params=_params(2),
        name="ln_in_projection",
    )(x2, mod4, w_in)


def _chunk_sum_matrices(tile, chunk):
    t = lax.broadcasted_iota(jnp.int32, (tile, tile), 0)
    u = lax.broadcasted_iota(jnp.int32, (tile, tile), 1)
    same = (t // chunk) == (u // chunk)
    n = tile // chunk
    cj = lax.broadcasted_iota(jnp.int32, (2 * n, tile), 0)
    cu = lax.broadcasted_iota(jnp.int32, (2 * n, tile), 1)
    is_total = cj < n
    in_chunk = (cu // chunk) == jnp.where(is_total, cj, cj - n)
    pos = cu % chunk

    def as_bf16(m):
        return jnp.where(m, 1.0, 0.0).astype(BF16)

    fw = (as_bf16(same & (u <= t)), as_bf16(in_chunk & (is_total | (pos <= chunk // 2 - 1))))
    bw = (as_bf16(same & (u >= t)), as_bf16(in_chunk & (is_total | (pos >= chunk // 2))))
    return fw, bw


def _rows_to_chunks(x, chunk):
    return jnp.concatenate([jnp.broadcast_to(x[j:j + 1], (chunk, x.shape[1])) for j in range(x.shape[0])],
                           axis=0)


def _hgrn_kernel(q_ref, i_ref, zf_ref, zb_ref, g_ref, lbl_ref, ng_ref, o_ref,
                 qa_s, ka_s, qi_s, u_s, acc_s, dec_s, *, layer):
    seq = q_ref.shape[1]
    dk = q_ref.shape[2]
    tile = MXU_DIM
    chunk = HG_CHUNK
    cpt = tile // chunk
    n_tiles = seq // tile
    n_chunks = seq // chunk

    lg = lbl_ref[...]
    e = jnp.exp(lg - jnp.max(lg, axis=1, keepdims=True))
    p = e / jnp.sum(e, axis=1, keepdims=True)
    if layer == 0:
        lbs = [jnp.zeros((1, dk), F32)] * 2
    else:
        lbs = [jnp.sum(p[d, 1:layer + 1, :], axis=0, keepdims=True) for d in range(2)]

    mats = _chunk_sum_matrices(tile, chunk)
    z_refs = (zf_ref, zb_ref)

    col_chunk = lax.broadcasted_iota(jnp.int32, (dk, tile), 1) // chunk

    def prep(t, carry):
        rows = slice(t * tile, (t + 1) * tile)
        q = q_ref[0, rows, :].astype(F32)
        vt =i_ref[0, rows, :].astype(F32).T.astype(BF16)
        ke = []
        for d in range(2):
            z = z_refs[d][0, rows, :].astype(F32)
            lb = lbs[d]
            f = lb + (1.0 - lb) * jax.nn.sigmoid(z)
            k = 1.0 - f
            lf = jnp.log(f)
            hi = lf.astype(BF16)
            lo = (lf - hi.astype(F32)).astype(BF16)
            x2 = jnp.concatenate([hi, lo], axis=1)
            incl, per_chunk = mats[d]
            r = jnp.dot(incl, x2, preferred_element_type=F32)
            bb = r[:, :dk] + r[:, dk:]
            r = jnp.dot(per_chunk, x2, preferred_element_type=F32)
            r = r[:, :dk] + r[:, dk:]
            tot = r[0:cpt]
            mid = r[cpt:2 * cpt]
            a1 = bb - _rows_to_chunks(mid, chunk)
            qa = q * jnp.exp(a1)
            ka = k * jnp.exp(-a1)
            lanes = slice(d * dk, (d + 1) * dk)
            qa_s[d, rows, :] = qa.astype(BF16)
            ka_s[d, rows, :] = ka.astype(BF16)
            qi_s[rows, lanes] = (qa * _rows_to_chunks(jnp.exp(mid), chunk)).astype(BF16)
            ke.append((ka * _rows_to_chunks(jnp.exp(tot - mid), chunk)).astype(BF16))
            dec_s[d, t * cpt:(t + 1) * cpt, :] = jnp.exp(tot)
        zero = jnp.zeros_like(vt)
        lhs = jnp.concatenate([jnp.where(col_chunk == j, vt, zero) for j in range(cpt)], axis=0)
        u = jnp.dot(lhs, jnp.concatenate(ke, axis=1), preferred_element_type=F32)
        for j in range(cpt):
            n = t * cpt + j
            u_s[:, n * 2 * dk:(n + 1) * 2 * dk] = u[j * dk:(j + 1) * dk]
        return carry

    for t in range(n_tiles):
        prep(t, 0)

    tt = lax.broadcasted_iota(jnp.int32, (tile, tile), 0)
    uu = lax.broadcasted_iota(jnp.int32, (tile, tile), 1)
    same = (tt // chunk) == (uu // chunk)
    causal = (same & (uu <= tt), same & (uu >= tt))
    ng = ng_ref[0]

    def direction_tile(d, t, s):
        rows = slice(t * tile, (t + 1) * tile)
        sc = lax.dot_general(qa_s[d, rows, :], ka_s[d, rows, :], NT_DIMS, preferred_element_type=F32)
        o = jnp.dot(jnp.where(causal[d], sc, 0.0).astype(BF16), i_ref[0, rows, :], preferred_element_type=F32)
        qi = qi_s[rows, d * dk:(d + 1) * dk]
        parts = [None] * cpt
        for j in (range(cpt) if d == 0 else reversed(range(cpt))):
            n = t * cpt + j
            parts[j] = lax.dot_general(qi[j * chunk:(j + 1) * chunk], s.astype(BF16), NT_DIMS,
                                       preferred_element_type=F32)
            s = dec_s[d, n:n + 1, :] * s + u_s[:, n * 2 * dk + d * dk:n * 2 * dk + (d + 1) * dk]
        return o + jnp.concatenate(parts, axis=0), s

    s = jnp.zeros((dk, dk), F32)
    for t in range(n_tiles):
        o, s = direction_tile(0, t, s)
        acc_s[t * tile:(t + 1) * tile, :] = o

    s = jnp.zeros((dk, dk), F32)
    for t in reversed(range(n_tiles)):
        rows = slice(t * tile, (t + 1) * tile)
        o, s = direction_tile(1, t, s)
        o = o + acc_s[rows, :]
        on = o * lax.rsqrt(jnp.mean(o * o, axis=-1, keepdims=True) + RMS_EPS) * ng
        g = g_ref[0, rows, :].astype(F32)
        o_ref[0, rows, :] = (on * (g * jax.nn.sigmoid(g))).astype(o_ref.dtype)


def _hgrn_branch(proj, lb_logits, norm_g, layer, width):
    bsz, seq, _ = proj.shape
    dk = HG_HEAD_DIM
    heads = width // dk
    depth = lb_logits.shape[1]
    n_chunks = seq // HG_CHUNK

    def col(group):
        return pl.BlockSpec((1, seq, dk), lambda b, h: (b, 0, group * heads + h))

    return pl.pallas_call(
        functools.partial(_hgrn_kernel, layer=layer),
        grid=(bsz, heads),
        in_specs=[col(0), col(1), col(2), col(3), col(4),
                  pl.BlockSpec((2, depth, dk), lambda b, h: (0, 0, h)),
                  pl.BlockSpec((1, 1, dk), lambda b, h: (layer, 0, 0))],
        out_specs=pl.BlockSpec((1, seq, dk), lambda b, h: (b, 0, h)),
        out_shape=jax.ShapeDtypeStruct((bsz, seq, width), BF16),
        scratch_shapes=[pltpu.VMEM((2, seq, dk), BF16),
                        pltpu.VMEM((2, seq, dk), BF16),
                        pltpu.VMEM((seq, 2 * dk), BF16),
                        pltpu.VMEM((dk, n_chunks * 2 * dk), F32),
                        pltpu.VMEM((seq, dk), F32),
                        pltpu.VMEM((2, n_chunks, dk), F32)],
        compiler_params=_params(2),
        name="hgrn2_branch",
    )(proj, proj, proj, proj, proj, lb_logits, norm_g.reshape(norm_g.shape[0], 1, dk))


def _na_bias_table(rpb_l):
    n_off = 2 * NA_KC - 1
    heads, n_rel = rpb_l.shape[0], rpb_l.shape[1]
    qc = np.arange(GRID_W)
    cs = np.clip(qc - NA_KC // 2, 0, GRID_W - NA_KC)
    kc = np.arange(GRID_W)
    valid = (kc[:, None] >= cs[None, :]) & (kc[:, None] < cs[None, :] + NA_KC)
    col_off = kc[:, None] - qc[None, :] + NA_KC - 1
    onehot = (col_off[None] == np.arange(n_off)[:, None, None]) & valid[None]
    sel = np.zeros((2, n_off, GRID_W, 2, GRID_W), np.float32)
    for h in range(2):
        sel[h, :, :, h, :] = onehot
    sel = sel.reshape(2 * n_off, GRID_W, 2 * GRID_W)
    valid2 = np.concatenate([valid, valid], axis=1)
    rpb_p = rpb_l.astype(F32).reshape(heads // 2, 2, n_rel, n_off).transpose(0, 2, 1, 3)
    rpb_p = rpb_p.reshape(heads // 2, n_rel, 2 * n_off)
    tz = jnp.einsum('prc,ckn->prkn', rpb_p, jnp.asarray(sel), precision=lax.Precision.HIGHEST)
    tz = jnp.where(jnp.asarray(valid2)[None, None], tz, MASK_VALUE)
    return tz.reshape(heads // 2, n_rel * GRID_W, 2 * GRID_W)


def _na_kernel(q_ref, k_ref, v_ref, g_ref, bias_ref, o_ref, vt_s, *, rows):
    w = GRID_W
    seq = rows * w
    kr_n = min(NA_ROWS_MAX, rows)
    half = kr_n // 2
    win = kr_n * w
    tile = MXU_DIM
    scale = NA_HEAD_DIM ** -0.5
    head0 = lax.broadcasted_iota(jnp.int32, (w, LANES), 1) < NA_HEAD_DIM

    for t in range(seq // tile):
        src0 = t * tile
        vt_s[0, :, src0:src0 + tile] = v_ref[0, src0:src0 + tile, :].astype(F32).T.astype(BF16)
        src1 = min(src0 + w, seq - tile)
        vt_s[1, :, src1 - w:src1 - w + tile] = v_ref[0, src1:src1 + tile, :].astype(F32).T.astype(BF16)

    for r in range(rows):
        rs = min(max(r - half, 0), rows - kr_n)
        rel0 = rs - r + (NA_ROWS_MAX - 1)
        brows = pl.ds(rel0 * w, win)
        qrows = pl.ds(r * w, w)
        krows = pl.ds(rs * w, win)
        par = rs % 2
        vcols = pl.ds((rs - par) * w, win)
        q = q_ref[0, qrows, :] * scale
        zq = jnp.zeros_like(q)
        qbd = jnp.concatenate([jnp.where(head0, q, zq), jnp.where(head0, zq, q)], axis=0)
        st = lax.dot_general(k_ref[0, krows, :], qbd, NT_DIMS, preferred_element_type=F32)
        st = st + bias_ref[0, brows, :]
        m = jnp.max(st, axis=0, keepdims=True)
        pt = jnp.exp(st - m)
        l = jnp.sum(pt, axis=0, keepdims=True)
        ot = jnp.dot(vt_s[par, :, vcols], pt.astype(BF16), preferred_element_type=F32) / l
        o2 = ot.T
        o = jnp.where(head0, o2[0:w], o2[w:2 * w])
        g = g_ref[0, qrows, :].astype(F32)
        o_ref[0, qrows, :] = (o * (g * jax.nn.sigmoid(g))).astype(o_ref.dtype)


def _na_branch(proj, bias, col0, width):
    bsz, seq, _ = proj.shape
    rows = seq // GRID_W
    pairs = width // LANES

    def col(group):
        return pl.BlockSpec((1, seq, LANES), lambda p, b: (b, 0, col0 + group * pairs + p))

    return pl.pallas_call(
        functools.partial(_na_kernel, rows=rows),
        grid=(pairs, bsz),
        in_specs=[col(0), col(1), col(2), col(3),
                  pl.BlockSpec((1,) + bias.shape[1:], lambda p, b: (p, 0, 0))],
        out_specs=pl.BlockSpec((1, seq, LANES), lambda p, b: (b, 0, p)),
        out_shape=jax.ShapeDtypeStruct((bsz, seq, width), BF16),
        scratch_shapes=[pltpu.VMEM((2, LANES, seq), BF16)],
        compiler_params=_params(2),
        name="neighbourhood_attention",
    )(proj, proj, proj, proj, bias)


def _mixer_kernel(hq_ref, hi_ref, hzf_ref, hzb_ref, hg_ref, lbl_ref, ng_ref,
                  nq_ref, nk_ref, nv_ref, ngate_ref, bias_ref, o_hg_ref, o_na_ref,
                  qa_s, ka_s, qi_s, u_s, acc_s, dec_s, vt_s, *, layer, rows):
    _hgrn_kernel(hq_ref, hi_ref, hzf_ref, hzb_ref, hg_ref, lbl_ref, ng_ref, o_hg_ref,
                 qa_s, ka_s, qi_s, u_s, acc_s, dec_s, layer=layer)
    _na_kernel(nq_ref, nk_ref, nv_ref, ngate_ref, bias_ref, o_na_ref, vt_s, rows=rows)


def _mixers(proj, lb_logits, norm_g, bias, layer, hg_width, na_width):
    bsz, seq, _ = proj.shape
    dk = HG_HEAD_DIM
    heads = hg_width // dk
    pairs = na_width // LANES
    assert heads == pairs and dk == LANES
    depth = lb_logits.shape[1]
    n_chunks = seq // HG_CHUNK
    na_col0 = 5 * heads

    def hg_col(group):
        return pl.BlockSpec((1, seq, dk), lambda b, h: (b, 0, group * heads + h))

    def na_col(group):
        return pl.BlockSpec((1, seq, LANES), lambda b, h: (b, 0, na_col0 + group * pairs + h))

    out_spec = pl.BlockSpec((1, seq, dk), lambda b, h: (b, 0, h))
    return pl.pallas_call(
        functools.partial(_mixer_kernel, layer=layer, rows=seq // GRID_W),
        grid=(bsz, heads),
        in_specs=[hg_col(0), hg_col(1), hg_col(2), hg_col(3), hg_col(4),
                  pl.BlockSpec((2, depth, dk), lambda b, h: (0, 0, h)),
                  pl.BlockSpec((1, 1, dk), lambda b, h: (layer, 0, 0)),
                  na_col(0), na_col(1), na_col(2), na_col(3),
                  pl.BlockSpec((1,) + bias.shape[1:], lambda b, h: (h, 0, 0))],
        out_specs=[out_spec, out_spec],
        out_shape=[jax.ShapeDtypeStruct((bsz, seq, hg_width), BF16),
                   jax.ShapeDtypeStruct((bsz, seq, na_width), BF16)],
        scratch_shapes=[pltpu.VMEM((2, seq, dk), BF16),
                        pltpu.VMEM((2, seq, dk), BF16),
                        pltpu.VMEM((seq, 2 * dk), BF16),
                        pltpu.VMEM((dk, n_chunks * 2 * dk), F32),
                        pltpu.VMEM((seq, dk), F32),
                        pltpu.VMEM((2, n_chunks, dk), F32),
                        pltpu.VMEM((2, LANES, seq), BF16)],
        compiler_params=_params(2),
        name="hgrn2_and_neighbourhood_attention",
    )(proj, proj, proj, proj, proj, lb_logits, norm_g.reshape(norm_g.shape[0], 1, dk),
      proj, proj, proj, proj, bias)


def _outproj_kernel(yh_ref, yn_ref, x_ref, mod_ref, w_ref, lg_ref, lb_ref, o_ref, *, d, alpha):
    hw, nw = yh_ref.shape[1], yn_ref.shape[1]
    gate = mod_ref[0, 0][:, 2 * d:3 * d]
    sub = yh_ref.shape[0] // OUT_ROW_SLICES
    for c in range(OUT_ROW_SLICES):
        rows = slice(c * sub, (c + 1) * sub)
        y = jnp.dot(yh_ref[rows, :], w_ref[0, 0:hw, :], preferred_element_type=F32)
        y = y + jnp.dot(yn_ref[rows, :], w_ref[0, hw:hw + nw, :], preferred_element_type=F32)
        z = alpha * x_ref[rows, :] + gate * y
        mu = jnp.mean(z, axis=-1, keepdims=True)
        zc = z - mu
        var = jnp.mean(zc * zc, axis=-1, keepdims=True)
        o_ref[rows, :] = zc * lax.rsqrt(var + LN_EPS) * lg_ref[0] + lb_ref[0]


def _out_projection(y_hg2, y_na2, x2, mod4, w_out_bf, ln_g, ln_b, layer, seq, tm, alpha):
    m, d = x2.shape
    hw, nw = y_hg2.shape[1], y_na2.shape[1]
    return pl.pallas_call(
        functools.partial(_outproj_kernel, d=d, alpha=alpha),
        grid=(m // tm,),
        in_specs=[pl.BlockSpec((tm, hw), lambda i: (i, 0)),
                  pl.BlockSpec((tm, nw), lambda i: (i, 0)),
                  pl.BlockSpec((tm, d), lambda i: (i, 0)),
                  pl.BlockSpec((1, 1, 1, 3 * d), lambda i: (layer, (i * tm) // seq, 0, 0)),
                  pl.BlockSpec((1, hw + nw, d), lambda i: (layer, 0, 0)),
                  pl.BlockSpec((1, 1, d), lambda i: (layer, 0, 0)),
                  pl.BlockSpec((1, 1, d), lambda i: (layer, 0, 0))],
        out_specs=pl.BlockSpec((tm, d), lambda i: (i, 0)),
        out_shape=jax.ShapeDtypeStruct((m, d), F32),
        compiler_params=_params(1),
        name="out_projection_deepnorm_ln",
    )(y_hg2, y_na2, x2, mod4, w_out_bf, ln_g.reshape(-1, 1, d), ln_b.reshape(-1, 1, d))


def _projection_tiles(m, seq, n_in):
    tm_in = seq
    tn_in = max(t for t in range(MXU_DIM, 4 * MXU_DIM + 1, MXU_DIM) if n_in % t == 0)
    tm_out = seq // 2
    assert m % tm_in == 0 and m % tm_out == 0 and seq % tm_out == 0
    return tm_in, tn_in, tm_out


def kernel(x, c, ada_w, ada_b, w_in, lb_logits, hg_norm_g, rpb, w_out, ln_g, ln_b):
    bsz, seq, d = x.shape
    depth = w_in.shape[0]
    mix = w_out.shape[1]
    hg_width = mix // 2
    na_width = mix - hg_width
    rows = seq // GRID_W
    alpha = (2 * depth) ** 0.25
    tm_in, tn_in, tm_out = _projection_tiles(bsz * seq, seq, w_in.shape[2])

    mod = _ada_modulation(c, ada_w, ada_b)
    mod4 = mod.reshape(depth, bsz, 1, 3 * d)
    w_out_bf = w_out.astype(BF16)
    na_col0 = 5 * hg_width // LANES

    x2 = x.reshape(bsz * seq, d)
    for layer in range(depth):
        proj = _in_projection(x2, mod4, w_in, layer, seq, tm_in, tn_in).reshape(bsz, seq, -1)
        y_hg, y_na = _mixers(proj, lb_logits, hg_norm_g, _na_bias_table(rpb[layer]), layer, hg_width, na_width)
        x2 = _out_projection(y_hg.reshape(bsz * seq, hg_width), y_na.reshape(bsz * seq, na_width),
                             x2, mod4, w_out_bf, ln_g, ln_b, layer, seq, tm_out, alpha)
    return x2.reshape(bsz, seq, d)
```

```python
import functools

import numpy as np
import jax
import jax.numpy as jnp
from jax import lax
from jax.experimental import pallas as pl
from jax.experimental.pallas import tpu as pltpu

GRID_W = 64
HG_HEAD_DIM = 128
HG_CHUNK = 32
NA_HEAD_DIM = 64
NA_ROWS_MAX = 8
NA_KC = 16
LN_EPS = 1e-5
RMS_EPS = 1e-6
MASK_VALUE = -1e30
OUT_ROW_SLICES = 4
IN_ROW_SLICES = 8

LANES = 128
MXU_DIM = 256
VMEM_LIMIT_BYTES = 56 * 1024 * 1024

F32 = jnp.float32
BF16 = jnp.bfloat16
NT_DIMS = (((1,), (1,)), ((), ()))


def _params(n_axes):
    return pltpu.CompilerParams(dimension_semantics=("arbitrary",) * n_axes,
                                vmem_limit_bytes=VMEM_LIMIT_BYTES)


def _ada_kernel(c_ref, w_ref, b_ref, o_ref):
    c = c_ref[...]
    a = c * jax.nn.sigmoid(c)
    o_ref[0] = jnp.dot(a, w_ref[0], precision=lax.Precision.HIGHEST,
                       preferred_element_type=F32) + b_ref[0]


def _ada_modulation(c, ada_w, ada_b):
    depth, d, cols = ada_w.shape
    bsz = c.shape[0]
    nj = cols // d
    return pl.pallas_call(
        _ada_kernel,
        grid=(depth, nj),
        in_specs=[pl.BlockSpec((bsz, d), lambda l, j: (0, 0)),
                  pl.BlockSpec((1, d, d), lambda l, j: (l, 0, j)),
                  pl.BlockSpec((1, 1, d), lambda l, j: (l, 0, j))],
        out_specs=pl.BlockSpec((1, bsz, d), lambda l, j: (l, 0, j)),
        out_shape=jax.ShapeDtypeStruct((depth, bsz, cols), F32),
        compiler_params=_params(2),
        name="ada_modulation",
    )(c, ada_w, ada_b.reshape(depth, 1, cols))


def _inproj_kernel(x_ref, mod_ref, w_ref, o_ref, h_ref, *, d):
    first = pl.program_id(1) == 0

    @pl.when(first)
    def _():
        w = w_ref[0].astype(BF16)
        m = mod_ref[0, 0]
        sub = x_ref.shape[0] // IN_ROW_SLICES
        for c in range(IN_ROW_SLICES):
            rows = slice(c * sub, (c + 1) * sub)
            x = x_ref[rows, :]
            mu = jnp.mean(x, axis=-1, keepdims=True)
            xc = x - mu
            var = jnp.mean(xc * xc, axis=-1, keepdims=True)
            h = (xc * lax.rsqrt(var + LN_EPS) * (1.0 + m[:, d:2 * d]) + m[:, 0:d]).astype(BF16)
            h_ref[rows, :] = h
            o_ref[rows, :] = jnp.dot(h, w, preferred_element_type=F32).astype(o_ref.dtype)

    @pl.when(jnp.logical_not(first))
    def _():
        o_ref[...] = jnp.dot(h_ref[...], w_ref[0].astype(BF16), preferred_element_type=F32).astype(o_ref.dtype)


def _in_projection(x2, mod4, w_in, layer, seq, tm, tn):
    m, d = x2.shape
    n = w_in.shape[2]
    return pl.pallas_call(
        functools.partial(_inproj_kernel, d=d),
        grid=(m // tm, n // tn),
        in_specs=[pl.BlockSpec((tm, d), lambda i, j: (i, 0)),
                  pl.BlockSpec((1, 1, 1, 3 * d), lambda i, j: (layer, (i * tm) // seq, 0, 0)),
                  pl.BlockSpec((1, d, tn), lambda i, j: (layer, 0, j))],
        out_specs=pl.BlockSpec((tm, tn), lambda i, j: (i, j)),
        out_shape=jax.ShapeDtypeStruct((m, n), BF16),
        scratch_shapes=[pltpu.VMEM((tm, d), BF16)],
        compiler_params=_params(2),
        name="ln_in_projection",
    )(x2, mod4, w_in)


def _chunk_sum_matrices(tile, chunk):
    t = lax.broadcasted_iota(jnp.int32, (tile, tile), 0)
    u = lax.broadcasted_iota(jnp.int32, (tile, tile), 1)
    same = (t // chunk) == (u // chunk)
    n = tile // chunk
    cj = lax.broadcasted_iota(jnp.int32, (2 * n, tile), 0)
    cu = lax.broadcasted_iota(jnp.int32, (2 * n, tile), 1)
    is_total = cj < n
    in_chunk = (cu // chunk) == jnp.where(is_total, cj, cj - n)
    pos = cu % chunk

    def as_bf16(m):
        return jnp.where(m, 1.0, 0.0).astype(BF16)

    fw = (as_bf16(same & (u <= t)), as_bf16(in_chunk & (is_total | (pos <= chunk // 2 - 1))))
    bw = (as_bf16(same & (u >= t)), as_bf16(in_chunk & (is_total | (pos >= chunk // 2))))
    return fw, bw


def _rows_to_chunks(x, chunk):
    return jnp.concatenate([jnp.broadcast_to(x[j:j + 1], (chunk, x.shape[1])) for j in range(x.shape[0])],
                           axis=0)


def _hgrn_kernel(q_ref, i_ref, zf_ref, zb_ref, g_ref, lbl_ref, ng_ref, o_ref,
                 qa_s, ka_s, qi_s, u_s, st_s, dec_s, *, layer):
    seq = q_ref.shape[1]
    dk = q_ref.shape[2]
    tile = MXU_DIM
    chunk = HG_CHUNK
    cpt = tile // chunk
    n_tiles = seq // tile
    n_chunks = seq // chunk

    lg = lbl_ref[...]
    e = jnp.exp(lg - jnp.max(lg, axis=1, keepdims=True))
    p = e / jnp.sum(e, axis=1, keepdims=True)
    if layer == 0:
        lbs = [jnp.zeros((1, dk), F32)] * 2
    else:
        lbs = [jnp.sum(p[d, 1:layer + 1, :], axis=0, keepdims=True) for d in range(2)]

    mats = _chunk_sum_matrices(tile, chunk)
    z_refs = (zf_ref, zb_ref)

    col_chunk = lax.broadcasted_iota(jnp.int32, (dk, tile), 1) // chunk

    def prep(t, carry):
        rows = slice(t * tile, (t + 1) * tile)
        q = q_ref[0, rows, :].astype(F32)
        vt =i_ref[0, rows, :].astype(F32).T.astype(BF16)
        ke = []
        for d in range(2):
            z = z_refs[d][0, rows, :].astype(F32)
            lb = lbs[d]
            f = lb + (1.0 - lb) * jax.nn.sigmoid(z)
            k = 1.0 - f
            lf = jnp.log(f)
            hi = lf.astype(BF16)
            lo = (lf - hi.astype(F32)).astype(BF16)
            x2 = jnp.concatenate([hi, lo], axis=1)
            incl, per_chunk = mats[d]
            r = jnp.dot(incl, x2, preferred_element_type=F32)
            bb = r[:, :dk] + r[:, dk:]
            r = jnp.dot(per_chunk, x2, preferred_element_type=F32)
            r = r[:, :dk] + r[:, dk:]
            tot = r[0:cpt]
            mid = r[cpt:2 * cpt]
            a1 = bb - _rows_to_chunks(mid, chunk)
            qa = q * jnp.exp(a1)
            ka = k * jnp.exp(-a1)
            lanes = slice(d * dk, (d + 1) * dk)
            qa_s[d, rows, :] = qa.astype(BF16)
            ka_s[d, rows, :] = ka.astype(BF16)
            qi_s[rows, lanes] = (qa * _rows_to_chunks(jnp.exp(mid), chunk)).astype(BF16)
            ke.append((ka * _rows_to_chunks(jnp.exp(tot - mid), chunk)).astype(BF16))
            dec_s[d, t * cpt:(t + 1) * cpt, :] = jnp.exp(tot)
        zero = jnp.zeros_like(vt)
        lhs = jnp.concatenate([jnp.where(col_chunk == j, vt, zero) for j in range(cpt)], axis=0)
        u = jnp.dot(lhs, jnp.concatenate(ke, axis=1), preferred_element_type=F32)
        for j in range(cpt):
            n = t * cpt + j
            u_s[:, n * 2 * dk:(n + 1) * 2 * dk] = u[j * dk:(j + 1) * dk]
        return carry

    def advance(d, n, s):
        st_s[n, d * dk:(d + 1) * dk, :] = s.T.astype(BF16)
        return dec_s[d, n:n + 1, :] * s + u_s[:, n * 2 * dk + d * dk:n * 2 * dk + (d + 1) * dk]

    s = jnp.zeros((dk, dk), F32)
    for t in range(n_tiles):
        prep(t, 0)
        for j in range(cpt):
            s = advance(0, t * cpt + j, s)
    s = jnp.zeros((dk, dk), F32)
    for n in reversed(range(n_chunks)):
        s = advance(1, n, s)

    tt = lax.broadcasted_iota(jnp.int32, (tile, tile), 0)
    uu = lax.broadcasted_iota(jnp.int32, (tile, tile), 1)
    same = (tt // chunk) == (uu // chunk)
    causal = (same & (uu <= tt), same & (uu >= tt))
    ng = ng_ref[0]

    for t in reversed(range(n_tiles)):
        rows = slice(t * tile, (t + 1) * tile)
        v = i_ref[0, rows, :]
        pm = []
        for d in range(2):
            sc = lax.dot_general(qa_s[d, rows, :], ka_s[d, rows, :], NT_DIMS, preferred_element_type=F32)
            pm.append(jnp.where(causal[d], sc, 0.0).astype(BF16))
        o = jnp.dot(jnp.concatenate(pm, axis=1), jnp.concatenate([v, v], axis=0), preferred_element_type=F32)
        qi = qi_s[rows, :]
        o = o + jnp.concatenate([jnp.dot(qi[j * chunk:(j + 1) * chunk], st_s[t * cpt + j],
                                         preferred_element_type=F32) for j in range(cpt)], axis=0)
        on = o * lax.rsqrt(jnp.mean(o * o, axis=-1, keepdims=True) + RMS_EPS) * ng
        g = g_ref[0, rows, :].astype(F32)
        o_ref[0, rows, :] = (on * (g * jax.nn.sigmoid(g))).astype(o_ref.dtype)


def _na_bias_table(rpb_l):
    n_off = 2 * NA_KC - 1
    heads, n_rel = rpb_l.shape[0], rpb_l.shape[1]
    qc = np.arange(GRID_W)
    cs = np.clip(qc - NA_KC // 2, 0, GRID_W - NA_KC)
    kc = np.arange(GRID_W)
    valid = (kc[:, None] >= cs[None, :]) & (kc[:, None] < cs[None, :] + NA_KC)
    col_off = kc[:, None] - qc[None, :] + NA_KC - 1
    onehot = (col_off[None] == np.arange(n_off)[:, None, None]) & valid[None]
    sel = np.zeros((2, n_off, GRID_W, 2, GRID_W), np.float32)
    for h in range(2):
        sel[h, :, :, h, :] = onehot
    sel = sel.reshape(2 * n_off, GRID_W, 2 * GRID_W)
    valid2 = np.concatenate([valid, valid], axis=1)
    rpb_p = rpb_l.astype(F32).reshape(heads // 2, 2, n_rel, n_off).transpose(0, 2, 1, 3)
    rpb_p = rpb_p.reshape(heads // 2, n_rel, 2 * n_off)
    tz = jnp.einsum('prc,ckn->prkn', rpb_p, jnp.asarray(sel), precision=lax.Precision.HIGHEST)
    tz = jnp.where(jnp.asarray(valid2)[None, None], tz, MASK_VALUE)
    return tz.reshape(heads // 2, n_rel * GRID_W, 2 * GRID_W)


def _na_kernel(q_ref, k_ref, v_ref, g_ref, bias_ref, o_ref, vt_s, *, rows):
    w = GRID_W
    seq = rows * w
    kr_n = min(NA_ROWS_MAX, rows)
    half = kr_n // 2
    win = kr_n * w
    tile = MXU_DIM
    scale = NA_HEAD_DIM ** -0.5
    head0 = lax.broadcasted_iota(jnp.int32, (w, LANES), 1) < NA_HEAD_DIM

    for t in range(seq // tile):
        src0 = t * tile
        vt_s[0, :, src0:src0 + tile] = v_ref[0, src0:src0 + tile, :].astype(F32).T.astype(BF16)
        src1 = min(src0 + w, seq - tile)
        vt_s[1, :, src1 - w:src1 - w + tile] = v_ref[0, src1:src1 + tile, :].astype(F32).T.astype(BF16)

    for r in range(rows):
        rs = min(max(r - half, 0), rows - kr_n)
        rel0 = rs - r + (NA_ROWS_MAX - 1)
        brows = pl.ds(rel0 * w, win)
        qrows = pl.ds(r * w, w)
        krows = pl.ds(rs * w, win)
        par = rs % 2
        vcols = pl.ds((rs - par) * w, win)
        q = q_ref[0, qrows, :] * scale
        zq = jnp.zeros_like(q)
        qbd = jnp.concatenate([jnp.where(head0, q, zq), jnp.where(head0, zq, q)], axis=0)
        st = lax.dot_general(k_ref[0, krows, :], qbd, NT_DIMS, preferred_element_type=F32)
        st = st + bias_ref[0, brows, :]
        m = jnp.max(st, axis=0, keepdims=True)
        pt = jnp.exp(st - m)
        l = jnp.sum(pt, axis=0, keepdims=True)
        ot = jnp.dot(vt_s[par, :, vcols], pt.astype(BF16), preferred_element_type=F32) / l
        o2 = ot.T
        o = jnp.where(head0, o2[0:w], o2[w:2 * w])
        g = g_ref[0, qrows, :].astype(F32)
        o_ref[0, qrows, :] = (o * (g * jax.nn.sigmoid(g))).astype(o_ref.dtype)


def _mixer_kernel(hq_ref, hi_ref, hzf_ref, hzb_ref, hg_ref, lbl_ref, ng_ref,
                  nq_ref, nk_ref, nv_ref, ngate_ref, bias_ref, o_hg_ref, o_na_ref,
                  qa_s, ka_s, qi_s, u_s, st_s, dec_s, vt_s, *, layer, rows):
    _hgrn_kernel(hq_ref, hi_ref, hzf_ref, hzb_ref, hg_ref, lbl_ref, ng_ref, o_hg_ref,
                 qa_s, ka_s, qi_s, u_s, st_s, dec_s, layer=layer)
    _na_kernel(nq_ref, nk_ref, nv_ref, ngate_ref, bias_ref, o_na_ref, vt_s, rows=rows)


def _mixers(proj, lb_logits, norm_g, bias, layer, hg_width, na_width):
    bsz, seq, _ = proj.shape
    dk = HG_HEAD_DIM
    heads = hg_width // dk
    pairs = na_width // LANES
    assert heads == pairs and dk == LANES
    depth = lb_logits.shape[1]
    n_chunks = seq // HG_CHUNK
    na_col0 = 5 * heads

    def hg_col(group):
        return pl.BlockSpec((1, seq, dk), lambda b, h: (b, 0, group * heads + h))

    def na_col(group):
        return pl.BlockSpec((1, seq, LANES), lambda b, h: (b, 0, na_col0 + group * pairs + h))

    out_spec = pl.BlockSpec((1, seq, dk), lambda b, h: (b, 0, h))
    return pl.pallas_call(
        functools.partial(_mixer_kernel, layer=layer, rows=seq // GRID_W),
        grid=(bsz, heads),
        in_specs=[hg_col(0), hg_col(1), hg_col(2), hg_col(3), hg_col(4),
                  pl.BlockSpec((2, depth, dk), lambda b, h: (0, 0, h)),
                  pl.BlockSpec((1, 1, dk), lambda b, h: (layer, 0, 0)),
                  na_col(0), na_col(1), na_col(2), na_col(3),
                  pl.BlockSpec((1,) + bias.shape[1:], lambda b, h: (h, 0, 0))],
        out_specs=[out_spec, out_spec],
        out_shape=[jax.ShapeDtypeStruct((bsz, seq, hg_width), BF16),
                   jax.ShapeDtypeStruct((bsz, seq, na_width), BF16)],
        scratch_shapes=[pltpu.VMEM((2, seq, dk), BF16),
                        pltpu.VMEM((2, seq, dk), BF16),
                        pltpu.VMEM((seq, 2 * dk), BF16),
                        pltpu.VMEM((dk, n_chunks * 2 * dk), F32),
                        pltpu.VMEM((n_chunks, 2 * dk, dk), BF16),
                        pltpu.VMEM((2, n_chunks, dk), F32),
                        pltpu.VMEM((2, LANES, seq), BF16)],
        compiler_params=_params(2),
        name="hgrn2_and_neighbourhood_attention",
    )(proj, proj, proj, proj, proj, lb_logits, norm_g.reshape(norm_g.shape[0], 1, dk),
      proj, proj, proj, proj, bias)


def _outproj_kernel(yh_ref, yn_ref, x_ref, mod_ref, w_ref, lg_ref, lb_ref, o_ref, *, d, alpha):
    hw, nw = yh_ref.shape[1], yn_ref.shape[1]
    gate = mod_ref[0, 0][:, 2 * d:3 * d]
    sub = yh_ref.shape[0] // OUT_ROW_SLICES
    for c in range(OUT_ROW_SLICES):
        rows = slice(c * sub, (c + 1) * sub)
        y = jnp.dot(yh_ref[rows, :], w_ref[0, 0:hw, :], preferred_element_type=F32)
        y = y + jnp.dot(yn_ref[rows, :], w_ref[0, hw:hw + nw, :], preferred_element_type=F32)
        z = alpha * x_ref[rows, :] + gate * y
        mu = jnp.mean(z, axis=-1, keepdims=True)
        zc = z - mu
        var = jnp.mean(zc * zc, axis=-1, keepdims=True)
        o_ref[rows, :] = zc * lax.rsqrt(var + LN_EPS) * lg_ref[0] + lb_ref[0]


def _out_projection(y_hg2, y_na2, x2, mod4, w_out_bf, ln_g, ln_b, layer, seq, tm, alpha):
    m, d = x2.shape
    hw, nw = y_hg2.shape[1], y_na2.shape[1]
    return pl.pallas_call(
        functools.partial(_outproj_kernel, d=d, alpha=alpha),
        grid=(m // tm,),
        in_specs=[pl.BlockSpec((tm, hw), lambda i: (i, 0)),
                  pl.BlockSpec((tm, nw), lambda i: (i, 0)),
                  pl.BlockSpec((tm, d), lambda i: (i, 0)),
                  pl.BlockSpec((1, 1, 1, 3 * d), lambda i: (layer, (i * tm) // seq, 0, 0)),
                  pl.BlockSpec((1, hw + nw, d), lambda i: (layer, 0, 0)),
                  pl.BlockSpec((1, 1, d), lambda i: (layer, 0, 0)),
                  pl.BlockSpec((1, 1, d), lambda i: (layer, 0, 0))],
        out_specs=pl.BlockSpec((tm, d), lambda i: (i, 0)),
        out_shape=jax.ShapeDtypeStruct((m, d), F32),
        compiler_params=_params(1),
        name="out_projection_deepnorm_ln",
    )(y_hg2, y_na2, x2, mod4, w_out_bf, ln_g.reshape(-1, 1, d), ln_b.reshape(-1, 1, d))


def _projection_tiles(m, seq, n_in):
    tm_in = seq
    tn_in = max(t for t in range(MXU_DIM, 4 * MXU_DIM + 1, MXU_DIM) if n_in % t == 0)
    tm_out = seq // 2
    assert m % tm_in == 0 and m % tm_out == 0 and seq % tm_out == 0
    return tm_in, tn_in, tm_out


def kernel(x, c, ada_w, ada_b, w_in, lb_logits, hg_norm_g, rpb, w_out, ln_g, ln_b):
    bsz, seq, d = x.shape
    depth = w_in.shape[0]
    mix = w_out.shape[1]
    hg_width = mix // 2
    na_width = mix - hg_width
    alpha = (2 * depth) ** 0.25
    tm_in, tn_in, tm_out = _projection_tiles(bsz * seq, seq, w_in.shape[2])

    mod = _ada_modulation(c, ada_w, ada_b)
    mod4 = mod.reshape(depth, bsz, 1, 3 * d)
    w_out_bf = w_out.astype(BF16)

    x2 = x.reshape(bsz * seq, d)
    for layer in range(depth):
        proj = _in_projection(x2, mod4, w_in, layer, seq, tm_in, tn_in).reshape(bsz, seq, -1)
        y_hg, y_na = _mixers(proj, lb_logits, hg_norm_g, _na_bias_table(rpb[layer]), layer, hg_width, na_width)
        x2 = _out_projection(y_hg.reshape(bsz * seq, hg_width), y_na.reshape(bsz * seq, na_width),
                             x2, mod4, w_out_bf, ln_g, ln_b, layer, seq, tm_out, alpha)
    return x2.reshape(bsz, seq, d)
```

```python
import functools

import numpy as np
import jax
import jax.numpy as jnp
from jax import lax
from jax.experimental import pallas as pl
from jax.experimental.pallas import tpu as pltpu

GRID_W = 64
HG_HEAD_DIM = 128
HG_CHUNK = 32
NA_HEAD_DIM = 64
NA_ROWS_MAX = 8
NA_KC = 16
LN_EPS = 1e-5
RMS_EPS = 1e-6
MASK_VALUE = -1e30
OUT_ROW_SLICES = 4
IN_ROW_SLICES = 8

LANES = 128
MXU_DIM = 256
VMEM_LIMIT_BYTES = 56 * 1024 * 1024

F32 = jnp.float32
BF16 = jnp.bfloat16
NT_DIMS = (((1,), (1,)), ((), ()))


def _params(n_axes):
    return pltpu.CompilerParams(dimension_semantics=("arbitrary",) * n_axes,
                                vmem_limit_bytes=VMEM_LIMIT_BYTES)


def _ada_kernel(c_ref, w_ref, b_ref, o_ref):
    c = c_ref[...]
    a = c * jax.nn.sigmoid(c)
    o_ref[0] = jnp.dot(a, w_ref[0], precision=lax.Precision.HIGHEST,
                       preferred_element_type=F32) + b_ref[0]


def _ada_modulation(c, ada_w, ada_b):
    depth, d, cols = ada_w.shape
    bsz = c.shape[0]
    nj = cols // d
    return pl.pallas_call(
        _ada_kernel,
        grid=(depth, nj),
        in_specs=[pl.BlockSpec((bsz, d), lambda l, j: (0, 0)),
                  pl.BlockSpec((1, d, d), lambda l, j: (l, 0, j)),
                  pl.BlockSpec((1, 1, d), lambda l, j: (l, 0, j))],
        out_specs=pl.BlockSpec((1, bsz, d), lambda l, j: (l, 0, j)),
        out_shape=jax.ShapeDtypeStruct((depth, bsz, cols), F32),
        compiler_params=_params(2),
        name="ada_modulation",
    )(c, ada_w, ada_b.reshape(depth, 1, cols))


def _inproj_kernel(x_ref, mod_ref, w_ref, o_ref, h_ref, *, d):
    first = pl.program_id(1) == 0

    @pl.when(first)
    def _():
        w = w_ref[0].astype(BF16)
        m = mod_ref[0, 0]
        sub = x_ref.shape[0] // IN_ROW_SLICES
        for c in range(IN_ROW_SLICES):
            rows = slice(c * sub, (c + 1) * sub)
            x = x_ref[rows, :]
            mu = jnp.mean(x, axis=-1, keepdims=True)
            xc = x - mu
            var = jnp.mean(xc * xc, axis=-1, keepdims=True)
            h = (xc * lax.rsqrt(var + LN_EPS) * (1.0 + m[:, d:2 * d]) + m[:, 0:d]).astype(BF16)
            h_ref[rows, :] = h
            o_ref[rows, :] = jnp.dot(h, w, preferred_element_type=F32).astype(o_ref.dtype)

    @pl.when(jnp.logical_not(first))
    def _():
        o_ref[...] = jnp.dot(h_ref[...], w_ref[0].astype(BF16), preferred_element_type=F32).astype(o_ref.dtype)


def _in_projection(x2, mod4, w_in, layer, seq, tm, tn):
    m, d = x2.shape
    n = w_in.shape[2]
    return pl.pallas_call(
        functools.partial(_inproj_kernel, d=d),
        grid=(m // tm, n // tn),
        in_specs=[pl.BlockSpec((tm, d), lambda i, j: (i, 0)),
                  pl.BlockSpec((1, 1, 1, 3 * d), lambda i, j: (layer, (i * tm) // seq, 0, 0)),
                  pl.BlockSpec((1, d, tn), lambda i, j: (layer, 0, j))],
        out_specs=pl.BlockSpec((tm, tn), lambda i, j: (i, j)),
        out_shape=jax.ShapeDtypeStruct((m, n), BF16),
        scratch_shapes=[pltpu.VMEM((tm, d), BF16)],
        compiler_params=_params(2),
        name="ln_in_projection",
    )(x2, mod4, w_in)


def _chunk_sum_matrices(tile, chunk):
    t = lax.broadcasted_iota(jnp.int32, (tile, tile), 0)
    u = lax.broadcasted_iota(jnp.int32, (tile, tile), 1)
    same = (t // chunk) == (u // chunk)
    n = tile // chunk
    cj = lax.broadcasted_iota(jnp.int32, (2 * n, tile), 0)
    cu = lax.broadcasted_iota(jnp.int32, (2 * n, tile), 1)
    is_total = cj < n
    in_chunk = (cu // chunk) == jnp.where(is_total, cj, cj - n)
    pos = cu % chunk

    def as_bf16(m):
        return jnp.where(m, 1.0, 0.0).astype(BF16)

    fw = (as_bf16(same & (u <= t)), as_bf16(in_chunk & (is_total | (pos <= chunk // 2 - 1))))
    bw = (as_bf16(same & (u >= t)), as_bf16(in_chunk & (is_total | (pos >= chunk // 2))))
    return fw, bw


def _rows_to_chunks(x, chunk):
    return jnp.concatenate([jnp.broadcast_to(x[j:j + 1], (chunk, x.shape[1])) for j in range(x.shape[0])],
                           axis=0)


def _hgrn_kernel(q_ref, i_ref, zf_ref, zb_ref, g_ref, lbl_ref, ng_ref, o_ref,
                 qa_s, ka_s, qi_s, u_s, st_s, dec_s, *, layer):
    seq = q_ref.shape[1]
    dk = q_ref.shape[2]
    tile = MXU_DIM
    chunk = HG_CHUNK
    cpt = tile // chunk
    n_tiles = seq // tile
    n_chunks = seq // chunk

    lg = lbl_ref[...]
    e = jnp.exp(lg - jnp.max(lg, axis=1, keepdims=True))
    p = e / jnp.sum(e, axis=1, keepdims=True)
    if layer == 0:
        lbs = [jnp.zeros((1, dk), F32)] * 2
    else:
        lbs = [jnp.sum(p[d, 1:layer + 1, :], axis=0, keepdims=True) for d in range(2)]

    mats = _chunk_sum_matrices(tile, chunk)
    z_refs = (zf_ref, zb_ref)

    col_chunk = lax.broadcasted_iota(jnp.int32, (dk, tile), 1) // chunk

    def prep(t, carry):
        rows = slice(t * tile, (t + 1) * tile)
        q = q_ref[0, rows, :].astype(F32)
        vt = i_ref[0, rows, :].T
        ke = []
        for d in range(2):
            z = z_refs[d][0, rows, :].astype(F32)
            lb = lbs[d]
            f = lb + (1.0 - lb) * jax.nn.sigmoid(z)
            k = 1.0 - f
            lf = jnp.log(f)
            hi = lf.astype(BF16)
            lo = (lf - hi.astype(F32)).astype(BF16)
            x2 = jnp.concatenate([hi, lo], axis=1)
            incl, per_chunk = mats[d]
            r = jnp.dot(incl, x2, preferred_element_type=F32)
            bb = r[:, :dk] + r[:, dk:]
            r = jnp.dot(per_chunk, x2, preferred_element_type=F32)
            r = r[:, :dk] + r[:, dk:]
            tot = r[0:cpt]
            mid = r[cpt:2 * cpt]
            a1 = bb - _rows_to_chunks(mid, chunk)
            qa = q * jnp.exp(a1)
            ka = k * jnp.exp(-a1)
            lanes = slice(d * dk, (d + 1) * dk)
            qa_s[d, rows, :] = qa.astype(BF16)
            ka_s[d, rows, :] = ka.astype(BF16)
            qi_s[rows, lanes] = (qa * _rows_to_chunks(jnp.exp(mid), chunk)).astype(BF16)
            ke.append((ka * _rows_to_chunks(jnp.exp(tot - mid), chunk)).astype(BF16))
            dec_s[d, t * cpt:(t + 1) * cpt, :] = jnp.exp(tot)
        zero = jnp.zeros_like(vt)
        lhs = jnp.concatenate([jnp.where(col_chunk == j, vt, zero) for j in range(cpt)], axis=0)
        u = jnp.dot(lhs, jnp.concatenate(ke, axis=1), preferred_element_type=F32)
        for j in range(cpt):
            n = t * cpt + j
            u_s[:, n * 2 * dk:(n + 1) * 2 * dk] = u[j * dk:(j + 1) * dk]
        return carry

    def advance(d, n, s):
        st_s[n, d * dk:(d + 1) * dk, :] = s.astype(BF16).T
        return dec_s[d, n:n + 1, :] * s + u_s[:, n * 2 * dk + d * dk:n * 2 * dk + (d + 1) * dk]

    s = jnp.zeros((dk, dk), F32)
    for t in range(n_tiles):
        prep(t, 0)
        for j in range(cpt):
            s = advance(0, t * cpt + j, s)
    s = jnp.zeros((dk, dk), F32)
    for n in reversed(range(n_chunks)):
        s = advance(1, n, s)

    tt = lax.broadcasted_iota(jnp.int32, (tile, tile), 0)
    uu = lax.broadcasted_iota(jnp.int32, (tile, tile), 1)
    same = (tt // chunk) == (uu // chunk)
    causal = (same & (uu <= tt), same & (uu >= tt))
    ng = ng_ref[0]

    for t in reversed(range(n_tiles)):
        rows = slice(t * tile, (t + 1) * tile)
        pm = None
        for d in range(2):
            sc = lax.dot_general(qa_s[d, rows, :], ka_s[d, rows, :], NT_DIMS, preferred_element_type=F32)
            sc = jnp.where(causal[d], sc, 0.0)
            pm = sc if pm is None else pm + sc
        o = jnp.dot(pm.astype(BF16), i_ref[0, rows, :], preferred_element_type=F32)
        qi = qi_s[rows, :]
        o = o + jnp.concatenate([jnp.dot(qi[j * chunk:(j + 1) * chunk], st_s[t * cpt + j],
                                         preferred_element_type=F32) for j in range(cpt)], axis=0)
        on = o * lax.rsqrt(jnp.mean(o * o, axis=-1, keepdims=True) + RMS_EPS) * ng
        g = g_ref[0, rows, :].astype(F32)
        o_ref[0, rows, :] = (on * (g * jax.nn.sigmoid(g))).astype(o_ref.dtype)


def _na_bias_table(rpb_l):
    n_off = 2 * NA_KC - 1
    heads, n_rel = rpb_l.shape[0], rpb_l.shape[1]
    qc = np.arange(GRID_W)
    cs = np.clip(qc - NA_KC // 2, 0, GRID_W - NA_KC)
    kc = np.arange(GRID_W)
    valid = (kc[:, None] >= cs[None, :]) & (kc[:, None] < cs[None, :] + NA_KC)
    col_off = kc[:, None] - qc[None, :] + NA_KC - 1
    onehot = (col_off[None] == np.arange(n_off)[:, None, None]) & valid[None]
    sel = np.zeros((2, n_off, GRID_W, 2, GRID_W), np.float32)
    for h in range(2):
        sel[h, :, :, h, :] = onehot
    sel = sel.reshape(2 * n_off, GRID_W, 2 * GRID_W)
    valid2 = np.concatenate([valid, valid], axis=1)
    rpb_p = rpb_l.astype(F32).reshape(heads // 2, 2, n_rel, n_off).transpose(0, 2, 1, 3)
    rpb_p = rpb_p.reshape(heads // 2, n_rel, 2 * n_off)
    tz = jnp.einsum('prc,ckn->prkn', rpb_p, jnp.asarray(sel), precision=lax.Precision.HIGHEST)
    tz = jnp.where(jnp.asarray(valid2)[None, None], tz, MASK_VALUE)
    return tz.reshape(heads // 2, n_rel * GRID_W, 2 * GRID_W)


def _na_kernel(q_ref, k_ref, v_ref, g_ref, bias_ref, o_ref, vt_s, *, rows):
    w = GRID_W
    seq = rows * w
    kr_n = min(NA_ROWS_MAX, rows)
    half = kr_n // 2
    win = kr_n * w
    tile = MXU_DIM
    scale = NA_HEAD_DIM ** -0.5
    head0 = lax.broadcasted_iota(jnp.int32, (w, LANES), 1) < NA_HEAD_DIM

    for t in range(seq // tile):
        src0 = t * tile
        vt_s[0, :, src0:src0 + tile] = v_ref[0, src0:src0 + tile, :].T
        src1 = min(src0 + w, seq - tile)
        vt_s[1, :, src1 - w:src1 - w + tile] = v_ref[0, src1:src1 + tile, :].T

    for r in range(rows):
        rs = min(max(r - half, 0), rows - kr_n)
        rel0 = rs - r + (NA_ROWS_MAX - 1)
        brows = pl.ds(rel0 * w, win)
        qrows = pl.ds(r * w, w)
        krows = pl.ds(rs * w, win)
        par = rs % 2
        vcols = pl.ds((rs - par) * w, win)
        q = q_ref[0, qrows, :] * scale
        zq = jnp.zeros_like(q)
        qbd = jnp.concatenate([jnp.where(head0, q, zq), jnp.where(head0, zq, q)], axis=0)
        st = lax.dot_general(k_ref[0, krows, :], qbd, NT_DIMS, preferred_element_type=F32)
        st = st + bias_ref[0, brows, :]
        m = jnp.max(st, axis=0, keepdims=True)
        pt = jnp.exp(st - m)
        l = jnp.sum(pt, axis=0, keepdims=True)
        ot = jnp.dot(vt_s[par, :, vcols], pt.astype(BF16), preferred_element_type=F32) / l
        o2 = ot.T
        o = jnp.where(head0, o2[0:w], o2[w:2 * w])
        g = g_ref[0, qrows, :].astype(F32)
        o_ref[0, qrows, :] = (o * (g * jax.nn.sigmoid(g))).astype(o_ref.dtype)


def _mixer_kernel(hq_ref, hi_ref, hzf_ref, hzb_ref, hg_ref, lbl_ref, ng_ref,
                  nq_ref, nk_ref, nv_ref, ngate_ref, bias_ref, o_hg_ref, o_na_ref,
                  qa_s, ka_s, qi_s, u_s, st_s, dec_s, vt_s, *, layer, rows):
    _hgrn_kernel(hq_ref, hi_ref, hzf_ref, hzb_ref, hg_ref, lbl_ref, ng_ref, o_hg_ref,
                 qa_s, ka_s, qi_s, u_s, st_s, dec_s, layer=layer)
    _na_kernel(nq_ref, nk_ref, nv_ref, ngate_ref, bias_ref, o_na_ref, vt_s, rows=rows)


def _mixers(proj, lb_logits, norm_g, bias, layer, hg_width, na_width):
    bsz, seq, _ = proj.shape
    dk = HG_HEAD_DIM
    heads = hg_width // dk
    pairs = na_width // LANES
    assert heads == pairs and dk == LANES
    depth = lb_logits.shape[1]
    n_chunks = seq // HG_CHUNK
    na_col0 = 5 * heads

    def hg_col(group):
        return pl.BlockSpec((1, seq, dk), lambda b, h: (b, 0, group * heads + h))

    def na_col(group):
        return pl.BlockSpec((1, seq, LANES), lambda b, h: (b, 0, na_col0 + group * pairs + h))

    out_spec = pl.BlockSpec((1, seq, dk), lambda b, h: (b, 0, h))
    return pl.pallas_call(
        functools.partial(_mixer_kernel, layer=layer, rows=seq // GRID_W),
        grid=(bsz, heads),
        in_specs=[hg_col(0), hg_col(1), hg_col(2), hg_col(3), hg_col(4),
                  pl.BlockSpec((2, depth, dk), lambda b, h: (0, 0, h)),
                  pl.BlockSpec((1, 1, dk), lambda b, h: (layer, 0, 0)),
                  na_col(0), na_col(1), na_col(2), na_col(3),
                  pl.BlockSpec((1,) + bias.shape[1:], lambda b, h: (h, 0, 0))],
        out_specs=[out_spec, out_spec],
        out_shape=[jax.ShapeDtypeStruct((bsz, seq, hg_width), BF16),
                   jax.ShapeDtypeStruct((bsz, seq, na_width), BF16)],
        scratch_shapes=[pltpu.VMEM((2, seq, dk), BF16),
                        pltpu.VMEM((2, seq, dk), BF16),
                        pltpu.VMEM((seq, 2 * dk), BF16),
                        pltpu.VMEM((dk, n_chunks * 2 * dk), F32),
                        pltpu.VMEM((n_chunks, 2 * dk, dk), BF16),
                        pltpu.VMEM((2, n_chunks, dk), F32),
                        pltpu.VMEM((2, LANES, seq), BF16)],
        compiler_params=_params(2),
        name="hgrn2_and_neighbourhood_attention",
    )(proj, proj, proj, proj, proj, lb_logits, norm_g.reshape(norm_g.shape[0], 1, dk),
      proj, proj, proj, proj, bias)


def _outproj_kernel(yh_ref, yn_ref, x_ref, mod_ref, w_ref, lg_ref, lb_ref, o_ref, *, d, alpha):
    hw, nw = yh_ref.shape[1], yn_ref.shape[1]
    gate = mod_ref[0, 0][:, 2 * d:3 * d]
    sub = yh_ref.shape[0] // OUT_ROW_SLICES
    for c in range(OUT_ROW_SLICES):
        rows = slice(c * sub, (c + 1) * sub)
        y = jnp.dot(yh_ref[rows, :], w_ref[0, 0:hw, :], preferred_element_type=F32)
        y = y + jnp.dot(yn_ref[rows, :], w_ref[0, hw:hw + nw, :], preferred_element_type=F32)
        z = alpha * x_ref[rows, :] + gate * y
        mu = jnp.mean(z, axis=-1, keepdims=True)
        zc = z - mu
        var = jnp.mean(zc * zc, axis=-1, keepdims=True)
        o_ref[rows, :] = zc * lax.rsqrt(var + LN_EPS) * lg_ref[0] + lb_ref[0]


def _out_projection(y_hg2, y_na2, x2, mod4, w_out_bf, ln_g, ln_b, layer, seq, tm, alpha):
    m, d = x2.shape
    hw, nw = y_hg2.shape[1], y_na2.shape[1]
    return pl.pallas_call(
        functools.partial(_outproj_kernel, d=d, alpha=alpha),
        grid=(m // tm,),
        in_specs=[pl.BlockSpec((tm, hw), lambda i: (i, 0)),
                  pl.BlockSpec((tm, nw), lambda i: (i, 0)),
                  pl.BlockSpec((tm, d), lambda i: (i, 0)),
                  pl.BlockSpec((1, 1, 1, 3 * d), lambda i: (layer, (i * tm) // seq, 0, 0)),
                  pl.BlockSpec((1, hw + nw, d), lambda i: (layer, 0, 0)),
                  pl.BlockSpec((1, 1, d), lambda i: (layer, 0, 0)),
                  pl.BlockSpec((1, 1, d), lambda i: (layer, 0, 0))],
        out_specs=pl.BlockSpec((tm, d), lambda i: (i, 0)),
        out_shape=jax.ShapeDtypeStruct((m, d), F32),
        compiler_params=_params(1),
        name="out_projection_deepnorm_ln",
    )(y_hg2, y_na2, x2, mod4, w_out_bf, ln_g.reshape(-1, 1, d), ln_b.reshape(-1, 1, d))


def _projection_tiles(m, seq, n_in):
    tm_in = seq
    tn_in = max(t for t in range(MXU_DIM, 4 * MXU_DIM + 1, MXU_DIM) if n_in % t == 0)
    tm_out = seq // 2
    assert m % tm_in == 0 and m % tm_out == 0 and seq % tm_out == 0
    return tm_in, tn_in, tm_out


def kernel(x, c, ada_w, ada_b, w_in, lb_logits, hg_norm_g, rpb, w_out, ln_g, ln_b):
    bsz, seq, d = x.shape
    depth = w_in.shape[0]
    mix = w_out.shape[1]
    hg_width = mix // 2
    na_width = mix - hg_width
    alpha = (2 * depth) ** 0.25
    tm_in, tn_in, tm_out = _projection_tiles(bsz * seq, seq, w_in.shape[2])

    mod = _ada_modulation(c, ada_w, ada_b)
    mod4 = mod.reshape(depth, bsz, 1, 3 * d)
    w_out_bf = w_out.astype(BF16)

    x2 = x.reshape(bsz * seq, d)
    for layer in range(depth):
        proj = _in_projection(x2, mod4, w_in, layer, seq, tm_in, tn_in).reshape(bsz, seq, -1)
        y_hg, y_na = _mixers(proj, lb_logits, hg_norm_g, _na_bias_table(rpb[layer]), layer, hg_width, na_width)
        x2 = _out_projection(y_hg.reshape(bsz * seq, hg_width), y_na.reshape(bsz * seq, na_width),
                             x2, mod4, w_out_bf, ln_g, ln_b, layer, seq, tm_out, alpha)
    return x2.reshape(bsz, seq, d)
```

```python
import functools

import numpy as np
import jax
import jax.numpy as jnp
from jax import lax
from jax.experimental import pallas as pl
from jax.experimental.pallas import tpu as pltpu

GRID_W = 64
HG_HEAD_DIM = 128
HG_CHUNK = 32
NA_HEAD_DIM = 64
NA_ROWS_MAX = 8
NA_KC = 16
LN_EPS = 1e-5
RMS_EPS = 1e-6
MASK_VALUE = -1e30
OUT_ROW_SLICES = 4
IN_ROW_SLICES = 8

LANES = 128
MXU_DIM = 256
VMEM_LIMIT_BYTES = 56 * 1024 * 1024

F32 = jnp.float32
BF16 = jnp.bfloat16
NT_DIMS = (((1,), (1,)), ((), ()))


def _params(n_axes):
    return pltpu.CompilerParams(dimension_semantics=("arbitrary",) * n_axes,
                                vmem_limit_bytes=VMEM_LIMIT_BYTES)


def _ada_kernel(c_ref, w_ref, b_ref, o_ref):
    c = c_ref[...]
    a = c * jax.nn.sigmoid(c)
    o_ref[0] = jnp.dot(a, w_ref[0], precision=lax.Precision.HIGHEST,
                       preferred_element_type=F32) + b_ref[0]


def _ada_modulation(c, ada_w, ada_b):
    depth, d, cols = ada_w.shape
    bsz = c.shape[0]
    nj = cols // d
    return pl.pallas_call(
        _ada_kernel,
        grid=(depth, nj),
        in_specs=[pl.BlockSpec((bsz, d), lambda l, j: (0, 0)),
                  pl.BlockSpec((1, d, d), lambda l, j: (l, 0, j)),
                  pl.BlockSpec((1, 1, d), lambda l, j: (l, 0, j))],
        out_specs=pl.BlockSpec((1, bsz, d), lambda l, j: (l, 0, j)),
        out_shape=jax.ShapeDtypeStruct((depth, bsz, cols), F32),
        compiler_params=_params(2),
        name="ada_modulation",
    )(c, ada_w, ada_b.reshape(depth, 1, cols))


def _inproj_kernel(x_ref, mod_ref, w_ref, o_ref, h_ref, *, d):
    first = pl.program_id(1) == 0

    @pl.when(first)
    def _():
        w = w_ref[0].astype(BF16)
        m = mod_ref[0, 0]
        sub = x_ref.shape[0] // IN_ROW_SLICES
        for c in range(IN_ROW_SLICES):
            rows = slice(c * sub, (c + 1) * sub)
            x = x_ref[rows, :]
            mu = jnp.mean(x, axis=-1, keepdims=True)
            xc = x - mu
            var = jnp.mean(xc * xc, axis=-1, keepdims=True)
            h = (xc * lax.rsqrt(var + LN_EPS) * (1.0 + m[:, d:2 * d]) + m[:, 0:d]).astype(BF16)
            h_ref[rows, :] = h
            o_ref[rows, :] = jnp.dot(h, w, preferred_element_type=F32).astype(o_ref.dtype)

    @pl.when(jnp.logical_not(first))
    def _():
        o_ref[...] = jnp.dot(h_ref[...], w_ref[0].astype(BF16), preferred_element_type=F32).astype(o_ref.dtype)


def _in_projection(x2, mod4, w_in, layer, seq, tm, tn):
    m, d = x2.shape
    n = w_in.shape[2]
    return pl.pallas_call(
        functools.partial(_inproj_kernel, d=d),
        grid=(m // tm, n // tn),
        in_specs=[pl.BlockSpec((tm, d), lambda i, j: (i, 0)),
                  pl.BlockSpec((1, 1, 1, 3 * d), lambda i, j: (layer, (i * tm) // seq, 0, 0)),
                  pl.BlockSpec((1, d, tn), lambda i, j: (layer, 0, j))],
        out_specs=pl.BlockSpec((tm, tn), lambda i, j: (i, j)),
        out_shape=jax.ShapeDtypeStruct((m, n), BF16),
        scratch_shapes=[pltpu.VMEM((tm, d), BF16)],
        compiler_params=_params(2),
        name="ln_in_projection",
    )(x2, mod4, w_in)


def _chunk_sum_matrices(tile, chunk):
    t = lax.broadcasted_iota(jnp.int32, (tile, tile), 0)
    u = lax.broadcasted_iota(jnp.int32, (tile, tile), 1)
    same = (t // chunk) == (u // chunk)
    n = tile // chunk
    cj = lax.broadcasted_iota(jnp.int32, (2 * n, tile), 0)
    cu = lax.broadcasted_iota(jnp.int32, (2 * n, tile), 1)
    is_total = cj < n
    in_chunk = (cu // chunk) == jnp.where(is_total, cj, cj - n)
    pos = cu % chunk

    def as_bf16(m):
        return jnp.where(m, 1.0, 0.0).astype(BF16)

    fw = (as_bf16(same & (u <= t)), as_bf16(in_chunk & (is_total | (pos <= chunk // 2 - 1))))
    bw = (as_bf16(same & (u >= t)), as_bf16(in_chunk & (is_total | (pos >= chunk // 2))))
    return fw, bw


def _rows_to_chunks(x, chunk):
    return jnp.concatenate([jnp.broadcast_to(x[j:j + 1], (chunk, x.shape[1])) for j in range(x.shape[0])],
                           axis=0)


def _hgrn_kernel(q_ref, i_ref, zf_ref, zb_ref, g_ref, lbl_ref, ng_ref, o_ref,
                 qa_s, ka_s, qi_s, u_s, st_s, dec_s, *, layer):
    seq = q_ref.shape[1]
    dk = q_ref.shape[2]
    tile = MXU_DIM
    chunk = HG_CHUNK
    cpt = tile // chunk
    n_tiles = seq // tile
    n_chunks = seq // chunk

    lg = lbl_ref[...]
    e = jnp.exp(lg - jnp.max(lg, axis=1, keepdims=True))
    p = e / jnp.sum(e, axis=1, keepdims=True)
    if layer == 0:
        lbs = [jnp.zeros((1, dk), F32)] * 2
    else:
        lbs = [jnp.sum(p[d, 1:layer + 1, :], axis=0, keepdims=True) for d in range(2)]

    mats = _chunk_sum_matrices(tile, chunk)
    z_refs = (zf_ref, zb_ref)

    col_chunk = lax.broadcasted_iota(jnp.int32, (dk, tile), 1) // chunk

    def prep(t, carry):
        rows = slice(t * tile, (t + 1) * tile)
        q = q_ref[0, rows, :].astype(F32)
        vt = i_ref[0, rows, :].T
        ke = []
        for d in range(2):
            z = z_refs[d][0, rows, :].astype(F32)
            lb = lbs[d]
            f = lb + (1.0 - lb) * jax.nn.sigmoid(z)
            k = 1.0 - f
            lf = jnp.log(f)
            hi = lf.astype(BF16)
            lo = (lf - hi.astype(F32)).astype(BF16)
            x2 = jnp.concatenate([hi, lo], axis=1)
            incl, per_chunk = mats[d]
            r = jnp.dot(incl, x2, preferred_element_type=F32)
            bb = r[:, :dk] + r[:, dk:]
            r = jnp.dot(per_chunk, x2, preferred_element_type=F32)
            r = r[:, :dk] + r[:, dk:]
            tot = r[0:cpt]
            mid = r[cpt:2 * cpt]
            a1 = bb - _rows_to_chunks(mid, chunk)
            qa = q * jnp.exp(a1)
            ka = k * jnp.exp(-a1)
            lanes = slice(d * dk, (d + 1) * dk)
            qa_s[d, rows, :] = qa.astype(BF16)
            ka_s[d, rows, :] = ka.astype(BF16)
            qi_s[rows, lanes] = (qa * _rows_to_chunks(jnp.exp(mid), chunk)).astype(BF16)
            ke.append((ka * _rows_to_chunks(jnp.exp(tot - mid), chunk)).astype(BF16))
            dec_s[d, t * cpt:(t + 1) * cpt, :] = jnp.exp(tot)
        zero = jnp.zeros_like(vt)
        lhs = jnp.concatenate([jnp.where(col_chunk == j, vt, zero) for j in range(cpt)], axis=0)
        u = jnp.dot(lhs, jnp.concatenate(ke, axis=1), preferred_element_type=F32)
        for j in range(cpt):
            n = t * cpt + j
            u_s[:, n * 2 * dk:(n + 1) * 2 * dk] = u[j * dk:(j + 1) * dk]
        return carry

    def advance(d, n, s):
        st_s[n, d * dk:(d + 1) * dk, :] = s.astype(BF16).T
        return dec_s[d, n:n + 1, :] * s + u_s[:, n * 2 * dk + d * dk:n * 2 * dk + (d + 1) * dk]

    s = jnp.zeros((dk, dk), F32)
    for t in range(n_tiles):
        prep(t, 0)
        for j in range(cpt):
            s = advance(0, t * cpt + j, s)
    s = jnp.zeros((dk, dk), F32)
    for n in reversed(range(n_chunks)):
        s = advance(1, n, s)

    tt = lax.broadcasted_iota(jnp.int32, (tile, tile), 0)
    uu = lax.broadcasted_iota(jnp.int32, (tile, tile), 1)
    same = (tt // chunk) == (uu // chunk)
    causal = (same & (uu <= tt), same & (uu >= tt))
    ng = ng_ref[0]

    for t in reversed(range(n_tiles)):
        rows = slice(t * tile, (t + 1) * tile)
        v = i_ref[0, rows, :]
        pm = []
        for d in range(2):
            sc = lax.dot_general(qa_s[d, rows, :], ka_s[d, rows, :], NT_DIMS, preferred_element_type=F32)
            pm.append(jnp.where(causal[d], sc, 0.0).astype(BF16))
        o = jnp.dot(jnp.concatenate(pm, axis=1), jnp.concatenate([v, v], axis=0), preferred_element_type=F32)
        qi = qi_s[rows, :]
        o = o + jnp.concatenate([jnp.dot(qi[j * chunk:(j + 1) * chunk], st_s[t * cpt + j],
                                         preferred_element_type=F32) for j in range(cpt)], axis=0)
        on = o * lax.rsqrt(jnp.mean(o * o, axis=-1, keepdims=True) + RMS_EPS) * ng
        g = g_ref[0, rows, :].astype(F32)
        o_ref[0, rows, :] = (on * (g * jax.nn.sigmoid(g))).astype(o_ref.dtype)


def _na_bias_table(rpb_l):
    n_off = 2 * NA_KC - 1
    heads, n_rel = rpb_l.shape[0], rpb_l.shape[1]
    qc = np.arange(GRID_W)
    cs = np.clip(qc - NA_KC // 2, 0, GRID_W - NA_KC)
    kc = np.arange(GRID_W)
    valid = (kc[:, None] >= cs[None, :]) & (kc[:, None] < cs[None, :] + NA_KC)
    col_off = kc[:, None] - qc[None, :] + NA_KC - 1
    onehot = (col_off[None] == np.arange(n_off)[:, None, None]) & valid[None]
    sel = np.zeros((2, n_off, GRID_W, 2, GRID_W), np.float32)
    for h in range(2):
        sel[h, :, :, h, :] = onehot
    sel = sel.reshape(2 * n_off, GRID_W, 2 * GRID_W)
    valid2 = np.concatenate([valid, valid], axis=1)
    rpb_p = rpb_l.astype(F32).reshape(heads // 2, 2, n_rel, n_off).transpose(0, 2, 1, 3)
    rpb_p = rpb_p.reshape(heads // 2, n_rel, 2 * n_off)
    tz = jnp.einsum('prc,ckn->prkn', rpb_p, jnp.asarray(sel), precision=lax.Precision.HIGHEST)
    tz = jnp.where(jnp.asarray(valid2)[None, None], tz, MASK_VALUE)
    return tz.reshape(heads // 2, n_rel * GRID_W, 2 * GRID_W)


def _na_kernel(q_ref, k_ref, v_ref, g_ref, bias_ref, o_ref, vt_s, *, rows):
    w = GRID_W
    seq = rows * w
    kr_n = min(NA_ROWS_MAX, rows)
    half = kr_n // 2
    win = kr_n * w
    tile = MXU_DIM
    scale = NA_HEAD_DIM ** -0.5
    head0 = lax.broadcasted_iota(jnp.int32, (w, LANES), 1) < NA_HEAD_DIM

    for t in range(seq // tile):
        src0 = t * tile
        vt_s[0, :, src0:src0 + tile] = v_ref[0, src0:src0 + tile, :].T
        src1 = min(src0 + w, seq - tile)
        vt_s[1, :, src1 - w:src1 - w + tile] = v_ref[0, src1:src1 + tile, :].T

    for r in range(rows):
        rs = min(max(r - half, 0), rows - kr_n)
        rel0 = rs - r + (NA_ROWS_MAX - 1)
        brows = pl.ds(rel0 * w, win)
        qrows = pl.ds(r * w, w)
        krows = pl.ds(rs * w, win)
        par = rs % 2
        vcols = pl.ds((rs - par) * w, win)
        q = q_ref[0, qrows, :] * scale
        zq = jnp.zeros_like(q)
        qbd = jnp.concatenate([jnp.where(head0, q, zq), jnp.where(head0, zq, q)], axis=0)
        st = lax.dot_general(k_ref[0, krows, :], qbd, NT_DIMS, preferred_element_type=F32)
        st = st + bias_ref[0, brows, :]
        m = jnp.max(st, axis=0, keepdims=True)
        pt = jnp.exp(st - m)
        l = jnp.sum(pt, axis=0, keepdims=True)
        ot = jnp.dot(vt_s[par, :, vcols], pt.astype(BF16), preferred_element_type=F32) / l
        o2 = ot.T
        o = jnp.where(head0, o2[0:w], o2[w:2 * w])
        g = g_ref[0, qrows, :].astype(F32)
        o_ref[0, qrows, :] = (o * (g * jax.nn.sigmoid(g))).astype(o_ref.dtype)


def _mixer_kernel(hq_ref, hi_ref, hzf_ref, hzb_ref, hg_ref, lbl_ref, ng_ref,
                  nq_ref, nk_ref, nv_ref, ngate_ref, bias_ref, o_hg_ref, o_na_ref,
                  qa_s, ka_s, qi_s, u_s, st_s, dec_s, vt_s, *, layer, rows):
    _hgrn_kernel(hq_ref, hi_ref, hzf_ref, hzb_ref, hg_ref, lbl_ref, ng_ref, o_hg_ref,
                 qa_s, ka_s, qi_s, u_s, st_s, dec_s, layer=layer)
    _na_kernel(nq_ref, nk_ref, nv_ref, ngate_ref, bias_ref, o_na_ref, vt_s, rows=rows)


def _mixers(proj, lb_logits, norm_g, bias, layer, hg_width, na_width):
    bsz, seq, _ = proj.shape
    dk = HG_HEAD_DIM
    heads = hg_width // dk
    pairs = na_width // LANES
    assert heads == pairs and dk == LANES
    depth = lb_logits.shape[1]
    n_chunks = seq // HG_CHUNK
    na_col0 = 5 * heads

    def hg_col(group):
        return pl.BlockSpec((1, seq, dk), lambda b, h: (b, 0, group * heads + h))

    def na_col(group):
        return pl.BlockSpec((1, seq, LANES), lambda b, h: (b, 0, na_col0 + group * pairs + h))

    out_spec = pl.BlockSpec((1, seq, dk), lambda b, h: (b, 0, h))
    return pl.pallas_call(
        functools.partial(_mixer_kernel, layer=layer, rows=seq // GRID_W),
        grid=(bsz, heads),
        in_specs=[hg_col(0), hg_col(1), hg_col(2), hg_col(3), hg_col(4),
                  pl.BlockSpec((2, depth, dk), lambda b, h: (0, 0, h)),
                  pl.BlockSpec((1, 1, dk), lambda b, h: (layer, 0, 0)),
                  na_col(0), na_col(1), na_col(2), na_col(3),
                  pl.BlockSpec((1,) + bias.shape[1:], lambda b, h: (h, 0, 0))],
        out_specs=[out_spec, out_spec],
        out_shape=[jax.ShapeDtypeStruct((bsz, seq, hg_width), BF16),
                   jax.ShapeDtypeStruct((bsz, seq, na_width), BF16)],
        scratch_shapes=[pltpu.VMEM((2, seq, dk), BF16),
                        pltpu.VMEM((2, seq, dk), BF16),
                        pltpu.VMEM((seq, 2 * dk), BF16),
                        pltpu.VMEM((dk, n_chunks * 2 * dk), F32),
                        pltpu.VMEM((n_chunks, 2 * dk, dk), BF16),
                        pltpu.VMEM((2, n_chunks, dk), F32),
                        pltpu.VMEM((2, LANES, seq), BF16)],
        compiler_params=_params(2),
        name="hgrn2_and_neighbourhood_attention",
    )(proj, proj, proj, proj, proj, lb_logits, norm_g.reshape(norm_g.shape[0], 1, dk),
      proj, proj, proj, proj, bias)


def _outproj_kernel(yh_ref, yn_ref, x_ref, mod_ref, w_ref, lg_ref, lb_ref, o_ref, *, d, alpha):
    hw, nw = yh_ref.shape[1], yn_ref.shape[1]
    gate = mod_ref[0, 0][:, 2 * d:3 * d]
    sub = yh_ref.shape[0] // OUT_ROW_SLICES
    for c in range(OUT_ROW_SLICES):
        rows = slice(c * sub, (c + 1) * sub)
        y = jnp.dot(yh_ref[rows, :], w_ref[0, 0:hw, :], preferred_element_type=F32)
        y = y + jnp.dot(yn_ref[rows, :], w_ref[0, hw:hw + nw, :], preferred_element_type=F32)
        z = alpha * x_ref[rows, :] + gate * y
        mu = jnp.mean(z, axis=-1, keepdims=True)
        zc = z - mu
        var = jnp.mean(zc * zc, axis=-1, keepdims=True)
        o_ref[rows, :] = zc * lax.rsqrt(var + LN_EPS) * lg_ref[0] + lb_ref[0]


def _out_projection(y_hg2, y_na2, x2, mod4, w_out_bf, ln_g, ln_b, layer, seq, tm, alpha):
    m, d = x2.shape
    hw, nw = y_hg2.shape[1], y_na2.shape[1]
    return pl.pallas_call(
        functools.partial(_outproj_kernel, d=d, alpha=alpha),
        grid=(m // tm,),
        in_specs=[pl.BlockSpec((tm, hw), lambda i: (i, 0)),
                  pl.BlockSpec((tm, nw), lambda i: (i, 0)),
                  pl.BlockSpec((tm, d), lambda i: (i, 0)),
                  pl.BlockSpec((1, 1, 1, 3 * d), lambda i: (layer, (i * tm) // seq, 0, 0)),
                  pl.BlockSpec((1, hw + nw, d), lambda i: (layer, 0, 0)),
                  pl.BlockSpec((1, 1, d), lambda i: (layer, 0, 0)),
                  pl.BlockSpec((1, 1, d), lambda i: (layer, 0, 0))],
        out_specs=pl.BlockSpec((tm, d), lambda i: (i, 0)),
        out_shape=jax.ShapeDtypeStruct((m, d), F32),
        compiler_params=_params(1),
        name="out_projection_deepnorm_ln",
    )(y_hg2, y_na2, x2, mod4, w_out_bf, ln_g.reshape(-1, 1, d), ln_b.reshape(-1, 1, d))


def _projection_tiles(m, seq, n_in):
    tm_in = seq
    tn_in = max(t for t in range(MXU_DIM, 4 * MXU_DIM + 1, MXU_DIM) if n_in % t == 0)
    tm_out = seq // 2
    assert m % tm_in == 0 and m % tm_out == 0 and seq % tm_out == 0
    return tm_in, tn_in, tm_out


def kernel(x, c, ada_w, ada_b, w_in, lb_logits, hg_norm_g, rpb, w_out, ln_g, ln_b):
    bsz, seq, d = x.shape
    depth = w_in.shape[0]
    mix = w_out.shape[1]
    hg_width = mix // 2
    na_width = mix - hg_width
    alpha = (2 * depth) ** 0.25
    tm_in, tn_in, tm_out = _projection_tiles(bsz * seq, seq, w_in.shape[2])

    mod = _ada_modulation(c, ada_w, ada_b)
    mod4 = mod.reshape(depth, bsz, 1, 3 * d)
    w_out_bf = w_out.astype(BF16)

    x2 = x.reshape(bsz * seq, d)
    for layer in range(depth):
        proj = _in_projection(x2, mod4, w_in, layer, seq, tm_in, tn_in).reshape(bsz, seq, -1)
        y_hg, y_na = _mixers(proj, lb_logits, hg_norm_g, _na_bias_table(rpb[layer]), layer, hg_width, na_width)
        x2 = _out_projection(y_hg.reshape(bsz * seq, hg_width), y_na.reshape(bsz * seq, na_width),
                             x2, mod4, w_out_bf, ln_g, ln_b, layer, seq, tm_out, alpha)
    return x2.reshape(bsz, seq, d)
```

```python
import functools

import numpy as np
import jax
import jax.numpy as jnp
from jax import lax
from jax.experimental import pallas as pl
from jax.experimental.pallas import tpu as pltpu

GRID_W = 64
HG_HEAD_DIM = 128
HG_CHUNK = 32
NA_HEAD_DIM = 64
NA_ROWS_MAX = 8
NA_KC = 16
LN_EPS = 1e-5
RMS_EPS = 1e-6
MASK_VALUE = -1e30
OUT_ROW_SLICES = 4
IN_ROW_SLICES = 8

LANES = 128
MXU_DIM = 256
VMEM_LIMIT_BYTES = 56 * 1024 * 1024

F32 = jnp.float32
BF16 = jnp.bfloat16
NT_DIMS = (((1,), (1,)), ((), ()))


def _params(n_axes):
    return pltpu.CompilerParams(dimension_semantics=("arbitrary",) * n_axes,
                                vmem_limit_bytes=VMEM_LIMIT_BYTES)


def _ada_kernel(c_ref, w_ref, b_ref, o_ref):
    c = c_ref[...]
    a = c * jax.nn.sigmoid(c)
    o_ref[0] = jnp.dot(a, w_ref[0], precision=lax.Precision.HIGHEST,
                       preferred_element_type=F32) + b_ref[0]


def _ada_modulation(c, ada_w, ada_b):
    depth, d, cols = ada_w.shape
    bsz = c.shape[0]
    nj = cols // d
    return pl.pallas_call(
        _ada_kernel,
        grid=(depth, nj),
        in_specs=[pl.BlockSpec((bsz, d), lambda l, j: (0, 0)),
                  pl.BlockSpec((1, d, d), lambda l, j: (l, 0, j)),
                  pl.BlockSpec((1, 1, d), lambda l, j: (l, 0, j))],
        out_specs=pl.BlockSpec((1, bsz, d), lambda l, j: (l, 0, j)),
        out_shape=jax.ShapeDtypeStruct((depth, bsz, cols), F32),
        compiler_params=_params(2),
        name="ada_modulation",
    )(c, ada_w, ada_b.reshape(depth, 1, cols))


def _inproj_kernel(x_ref, mod_ref, w_ref, o_ref, h_ref, *, d):
    first = pl.program_id(1) == 0

    @pl.when(first)
    def _():
        w = w_ref[0].astype(BF16)
        m = mod_ref[0, 0]
        sub = x_ref.shape[0] // IN_ROW_SLICES
        for c in range(IN_ROW_SLICES):
            rows = slice(c * sub, (c + 1) * sub)
            x = x_ref[rows, :]
            mu = jnp.mean(x, axis=-1, keepdims=True)
            xc = x - mu
            var = jnp.mean(xc * xc, axis=-1, keepdims=True)
            h = (xc * lax.rsqrt(var + LN_EPS) * (1.0 + m[:, d:2 * d]) + m[:, 0:d]).astype(BF16)
            h_ref[rows, :] = h
            o_ref[rows, :] = jnp.dot(h, w, preferred_element_type=F32).astype(o_ref.dtype)

    @pl.when(jnp.logical_not(first))
    def _():
        o_ref[...] = jnp.dot(h_ref[...], w_ref[0].astype(BF16), preferred_element_type=F32).astype(o_ref.dtype)


def _in_projection(x2, mod4, w_in, layer, seq, tm, tn):
    m, d = x2.shape
    n = w_in.shape[2]
    return pl.pallas_call(
        functools.partial(_inproj_kernel, d=d),
        grid=(m // tm, n // tn),
        in_specs=[pl.BlockSpec((tm, d), lambda i, j: (i, 0)),
                  pl.BlockSpec((1, 1, 1, 3 * d), lambda i, j: (layer, (i * tm) // seq, 0, 0)),
                  pl.BlockSpec((1, d, tn), lambda i, j: (layer, 0, j))],
        out_specs=pl.BlockSpec((tm, tn), lambda i, j: (i, j)),
        out_shape=jax.ShapeDtypeStruct((m, n), BF16),
        scratch_shapes=[pltpu.VMEM((tm, d), BF16)],
        compiler_params=_params(2),
        name="ln_in_projection",
    )(x2, mod4, w_in)


def _chunk_sum_matrices(tile, chunk):
    t = lax.broadcasted_iota(jnp.int32, (tile, tile), 0)
    u = lax.broadcasted_iota(jnp.int32, (tile, tile), 1)
    same = (t // chunk) == (u // chunk)
    n = tile // chunk
    cj = lax.broadcasted_iota(jnp.int32, (2 * n, tile), 0)
    cu = lax.broadcasted_iota(jnp.int32, (2 * n, tile), 1)
    is_total = cj < n
    in_chunk = (cu // chunk) == jnp.where(is_total, cj, cj - n)
    pos = cu % chunk

    def as_bf16(m):
        return jnp.where(m, 1.0, 0.0).astype(BF16)

    fw = (as_bf16(same & (u <= t)), as_bf16(in_chunk & (is_total | (pos <= chunk // 2 - 1))))
    bw = (as_bf16(same & (u >= t)), as_bf16(in_chunk & (is_total | (pos >= chunk // 2))))
    return fw, bw


def _rows_to_chunks(x, chunk):
    return jnp.concatenate([jnp.broadcast_to(x[j:j + 1], (chunk, x.shape[1])) for j in range(x.shape[0])],
                           axis=0)


def _hgrn_kernel(q_ref, i_ref, zf_ref, zb_ref, g_ref, lbl_ref, ng_ref, o_ref,
                 qa_s, ka_s, qi_s, u_s, st_s, dec_s, *, layer):
    seq = q_ref.shape[1]
    dk = q_ref.shape[2]
    tile = MXU_DIM
    chunk = HG_CHUNK
    cpt = tile // chunk
    n_tiles = seq // tile
    n_chunks = seq // chunk

    lg = lbl_ref[...]
    e = jnp.exp(lg - jnp.max(lg, axis=1, keepdims=True))
    p = e / jnp.sum(e, axis=1, keepdims=True)
    if layer == 0:
        lbs = [jnp.zeros((1, dk), F32)] * 2
    else:
        lbs = [jnp.sum(p[d, 1:layer + 1, :], axis=0, keepdims=True) for d in range(2)]

    mats = _chunk_sum_matrices(tile, chunk)
    z_refs = (zf_ref, zb_ref)

    col_chunk = lax.broadcasted_iota(jnp.int32, (dk, tile), 1) // chunk

    def prep(t):
        rows = slice(t * tile, (t + 1) * tile)
        q = q_ref[0, rows, :].astype(F32)
        vt = i_ref[0, rows, :].T
        ke = []
        for d in range(2):
            z = z_refs[d][0, rows, :].astype(F32)
            lb = lbs[d]
            f = lb + (1.0 - lb) * jax.nn.sigmoid(z)
            k = 1.0 - f
            lf = jnp.log(f)
            hi = lf.astype(BF16)
            lo = (lf - hi.astype(F32)).astype(BF16)
            x2 = jnp.concatenate([hi, lo], axis=1)
            incl, per_chunk = mats[d]
            r = jnp.dot(incl, x2, preferred_element_type=F32)
            bb = r[:, :dk] + r[:, dk:]
            r = jnp.dot(per_chunk, x2, preferred_element_type=F32)
            r = r[:, :dk] + r[:, dk:]
            tot = r[0:cpt]
            mid = r[cpt:2 * cpt]
            a1 = bb - _rows_to_chunks(mid, chunk)
            qa = q * jnp.exp(a1)
            ka = k * jnp.exp(-a1)
            lanes = slice(d * dk, (d + 1) * dk)
            qa_s[d, rows, :] = qa.astype(BF16)
            ka_s[d, rows, :] = ka.astype(BF16)
            qi_s[rows, lanes] = (qa * _rows_to_chunks(jnp.exp(mid), chunk)).astype(BF16)
            ke.append((ka * _rows_to_chunks(jnp.exp(tot - mid), chunk)).astype(BF16))
            dec_s[d, t * cpt:(t + 1) * cpt, :] = jnp.exp(tot)
        zero = jnp.zeros_like(vt)
        lhs = jnp.concatenate([jnp.where(col_chunk == j, vt, zero) for j in range(cpt)], axis=0)
        u = jnp.dot(lhs, jnp.concatenate(ke, axis=1), preferred_element_type=F32)
        for j in range(cpt):
            n = t * cpt + j
            u_s[:, n * 2 * dk:(n + 1) * 2 * dk] = u[j * dk:(j + 1) * dk]

    def advance(d, n, s):
        st_s[n, d * dk:(d + 1) * dk, :] = s.astype(BF16).T
        return dec_s[d, n:n + 1, :] * s + u_s[:, n * 2 * dk + d * dk:n * 2 * dk + (d + 1) * dk]

    s = jnp.zeros((dk, dk), F32)
    for t in range(n_tiles):
        prep(t)
        for j in range(cpt):
            s = advance(0, t * cpt + j, s)
    s = jnp.zeros((dk, dk), F32)
    for n in reversed(range(n_chunks)):
        s = advance(1, n, s)

    tt = lax.broadcasted_iota(jnp.int32, (tile, tile), 0)
    uu = lax.broadcasted_iota(jnp.int32, (tile, tile), 1)
    same = (tt // chunk) == (uu // chunk)
    causal = (same & (uu <= tt), same & (uu >= tt))
    ng = ng_ref[0]

    for t in reversed(range(n_tiles)):
        rows = slice(t * tile, (t + 1) * tile)
        v = i_ref[0, rows, :]
        pm = []
        for d in range(2):
            sc = lax.dot_general(qa_s[d, rows, :], ka_s[d, rows, :], NT_DIMS, preferred_element_type=F32)
            pm.append(jnp.where(causal[d], sc, 0.0).astype(BF16))
        o = jnp.dot(jnp.concatenate(pm, axis=1), jnp.concatenate([v, v], axis=0), preferred_element_type=F32)
        qi = qi_s[rows, :]
        o = o + jnp.concatenate([jnp.dot(qi[j * chunk:(j + 1) * chunk], st_s[t * cpt + j],
                                         preferred_element_type=F32) for j in range(cpt)], axis=0)
        on = o * lax.rsqrt(jnp.mean(o * o, axis=-1, keepdims=True) + RMS_EPS) * ng
        g = g_ref[0, rows, :].astype(F32)
        o_ref[0, rows, :] = (on * (g * jax.nn.sigmoid(g))).astype(o_ref.dtype)


def _na_bias_table(rpb_l):
    n_off = 2 * NA_KC - 1
    heads, n_rel = rpb_l.shape[0], rpb_l.shape[1]
    qc = np.arange(GRID_W)
    cs = np.clip(qc - NA_KC // 2, 0, GRID_W - NA_KC)
    kc = np.arange(GRID_W)
    valid = (kc[:, None] >= cs[None, :]) & (kc[:, None] < cs[None, :] + NA_KC)
    col_off = kc[:, None] - qc[None, :] + NA_KC - 1
    onehot = (col_off[None] == np.arange(n_off)[:, None, None]) & valid[None]
    sel = np.zeros((2, n_off, GRID_W, 2, GRID_W), np.float32)
    for h in range(2):
        sel[h, :, :, h, :] = onehot
    sel = sel.reshape(2 * n_off, GRID_W, 2 * GRID_W)
    valid2 = np.concatenate([valid, valid], axis=1)
    rpb_p = rpb_l.astype(F32).reshape(heads // 2, 2, n_rel, n_off).transpose(0, 2, 1, 3)
    rpb_p = rpb_p.reshape(heads // 2, n_rel, 2 * n_off)
    tz = jnp.einsum('prc,ckn->prkn', rpb_p, jnp.asarray(sel), precision=lax.Precision.HIGHEST)
    tz = jnp.where(jnp.asarray(valid2)[None, None], tz, MASK_VALUE)
    return tz.reshape(heads // 2, n_rel * GRID_W, 2 * GRID_W)


def _na_kernel(q_ref, k_ref, v_ref, g_ref, bias_ref, o_ref, vt_s, *, rows):
    w = GRID_W
    seq = rows * w
    kr_n = min(NA_ROWS_MAX, rows)
    half = kr_n // 2
    win = kr_n * w
    tile = MXU_DIM
    scale = NA_HEAD_DIM ** -0.5
    head0 = lax.broadcasted_iota(jnp.int32, (w, LANES), 1) < NA_HEAD_DIM

    for t in range(seq // tile):
        src0 = t * tile
        vt_s[0, :, src0:src0 + tile] = v_ref[0, src0:src0 + tile, :].T
        src1 = min(src0 + w, seq - tile)
        vt_s[1, :, src1 - w:src1 - w + tile] = v_ref[0, src1:src1 + tile, :].T

    for r in range(rows):
        rs = min(max(r - half, 0), rows - kr_n)
        rel0 = rs - r + (NA_ROWS_MAX - 1)
        brows = pl.ds(rel0 * w, win)
        qrows = pl.ds(r * w, w)
        krows = pl.ds(rs * w, win)
        par = rs % 2
        vcols = pl.ds((rs - par) * w, win)
        q = q_ref[0, qrows, :] * scale
        zq = jnp.zeros_like(q)
        qbd = jnp.concatenate([jnp.where(head0, q, zq), jnp.where(head0, zq, q)], axis=0)
        st = lax.dot_general(k_ref[0, krows, :], qbd, NT_DIMS, preferred_element_type=F32)
        st = st + bias_ref[0, brows, :]
        m = jnp.max(st, axis=0, keepdims=True)
        pt = jnp.exp(st - m)
        l = jnp.sum(pt, axis=0, keepdims=True)
        ot = jnp.dot(vt_s[par, :, vcols], pt.astype(BF16), preferred_element_type=F32) / l
        o2 = ot.T
        o = jnp.where(head0, o2[0:w], o2[w:2 * w])
        g = g_ref[0, qrows, :].astype(F32)
        o_ref[0, qrows, :] = (o * (g * jax.nn.sigmoid(g))).astype(o_ref.dtype)


def _mixer_kernel(hq_ref, hi_ref, hzf_ref, hzb_ref, hg_ref, lbl_ref, ng_ref,
                  nq_ref, nk_ref, nv_ref, ngate_ref, bias_ref, o_hg_ref, o_na_ref,
                  qa_s, ka_s, qi_s, u_s, st_s, dec_s, vt_s, *, layer, rows):
    _hgrn_kernel(hq_ref, hi_ref, hzf_ref, hzb_ref, hg_ref, lbl_ref, ng_ref, o_hg_ref,
                 qa_s, ka_s, qi_s, u_s, st_s, dec_s, layer=layer)
    _na_kernel(nq_ref, nk_ref, nv_ref, ngate_ref, bias_ref, o_na_ref, vt_s, rows=rows)


def _mixers(proj, lb_logits, norm_g, bias, layer, hg_width, na_width):
    bsz, seq, _ = proj.shape
    dk = HG_HEAD_DIM
    heads = hg_width // dk
    pairs = na_width // LANES
    assert heads == pairs and dk == LANES
    depth = lb_logits.shape[1]
    n_chunks = seq // HG_CHUNK
    na_col0 = 5 * heads

    def hg_col(group):
        return pl.BlockSpec((1, seq, dk), lambda b, h: (b, 0, group * heads + h))

    def na_col(group):
        return pl.BlockSpec((1, seq, LANES), lambda b, h: (b, 0, na_col0 + group * pairs + h))

    out_spec = pl.BlockSpec((1, seq, dk), lambda b, h: (b, 0, h))
    return pl.pallas_call(
        functools.partial(_mixer_kernel, layer=layer, rows=seq // GRID_W),
        grid=(bsz, heads),
        in_specs=[hg_col(0), hg_col(1), hg_col(2), hg_col(3), hg_col(4),
                  pl.BlockSpec((2, depth, dk), lambda b, h: (0, 0, h)),
                  pl.BlockSpec((1, 1, dk), lambda b, h: (layer, 0, 0)),
                  na_col(0), na_col(1), na_col(2), na_col(3),
                  pl.BlockSpec((1,) + bias.shape[1:], lambda b, h: (h, 0, 0))],
        out_specs=[out_spec, out_spec],
        out_shape=[jax.ShapeDtypeStruct((bsz, seq, hg_width), BF16),
                   jax.ShapeDtypeStruct((bsz, seq, na_width), BF16)],
        scratch_shapes=[pltpu.VMEM((2, seq, dk), BF16),
                        pltpu.VMEM((2, seq, dk), BF16),
                        pltpu.VMEM((seq, 2 * dk), BF16),
                        pltpu.VMEM((dk, n_chunks * 2 * dk), F32),
                        pltpu.VMEM((n_chunks, 2 * dk, dk), BF16),
                        pltpu.VMEM((2, n_chunks, dk), F32),
                        pltpu.VMEM((2, LANES, seq), BF16)],
        compiler_params=_params(2),
        name="hgrn2_and_neighbourhood_attention",
    )(proj, proj, proj, proj, proj, lb_logits, norm_g.reshape(norm_g.shape[0], 1, dk),
      proj, proj, proj, proj, bias)


def _outproj_kernel(yh_ref, yn_ref, x_ref, mod_ref, w_ref, lg_ref, lb_ref, o_ref, *, d, alpha):
    hw, nw = yh_ref.shape[1], yn_ref.shape[1]
    gate = mod_ref[0, 0][:, 2 * d:3 * d]
    sub = yh_ref.shape[0] // OUT_ROW_SLICES
    for c in range(OUT_ROW_SLICES):
        rows = slice(c * sub, (c + 1) * sub)
        y = jnp.dot(yh_ref[rows, :], w_ref[0, 0:hw, :], preferred_element_type=F32)
        y = y + jnp.dot(yn_ref[rows, :], w_ref[0, hw:hw + nw, :], preferred_element_type=F32)
        z = alpha * x_ref[rows, :] + gate * y
        mu = jnp.mean(z, axis=-1, keepdims=True)
        zc = z - mu
        var = jnp.mean(zc * zc, axis=-1, keepdims=True)
        o_ref[rows, :] = zc * lax.rsqrt(var + LN_EPS) * lg_ref[0] + lb_ref[0]


def _out_projection(y_hg2, y_na2, x2, mod4, w_out_bf, ln_g, ln_b, layer, seq, tm, alpha):
    m, d = x2.shape
    hw, nw = y_hg2.shape[1], y_na2.shape[1]
    return pl.pallas_call(
        functools.partial(_outproj_kernel, d=d, alpha=alpha),
        grid=(m // tm,),
        in_specs=[pl.BlockSpec((tm, hw), lambda i: (i, 0)),
                  pl.BlockSpec((tm, nw), lambda i: (i, 0)),
                  pl.BlockSpec((tm, d), lambda i: (i, 0)),
                  pl.BlockSpec((1, 1, 1, 3 * d), lambda i: (layer, (i * tm) // seq, 0, 0)),
                  pl.BlockSpec((1, hw + nw, d), lambda i: (layer, 0, 0)),
                  pl.BlockSpec((1, 1, d), lambda i: (layer, 0, 0)),
                  pl.BlockSpec((1, 1, d), lambda i: (layer, 0, 0))],
        out_specs=pl.BlockSpec((tm, d), lambda i: (i, 0)),
        out_shape=jax.ShapeDtypeStruct((m, d), F32),
        compiler_params=_params(1),
        name="out_projection_deepnorm_ln",
    )(y_hg2, y_na2, x2, mod4, w_out_bf, ln_g.reshape(-1, 1, d), ln_b.reshape(-1, 1, d))


def _projection_tiles(m, seq, n_in):
    tm_in = seq
    tn_in = max(t for t in range(MXU_DIM, 4 * MXU_DIM + 1, MXU_DIM) if n_in % t == 0)
    tm_out = seq // 2
    assert m % tm_in == 0 and m % tm_out == 0 and seq % tm_out == 0
    return tm_in, tn_in, tm_out


def kernel(x, c, ada_w, ada_b, w_in, lb_logits, hg_norm_g, rpb, w_out, ln_g, ln_b):
    bsz, seq, d = x.shape
    depth = w_in.shape[0]
    mix = w_out.shape[1]
    hg_width = mix // 2
    na_width = mix - hg_width
    alpha = (2 * depth) ** 0.25
    tm_in, tn_in, tm_out = _projection_tiles(bsz * seq, seq, w_in.shape[2])

    mod = _ada_modulation(c, ada_w, ada_b)
    mod4 = mod.reshape(depth, bsz, 1, 3 * d)
    w_out_bf = w_out.astype(BF16)

    x2 = x.reshape(bsz * seq, d)
    for layer in range(depth):
        proj = _in_projection(x2, mod4, w_in, layer, seq, tm_in, tn_in).reshape(bsz, seq, -1)
        y_hg, y_na = _mixers(proj, lb_logits, hg_norm_g, _na_bias_table(rpb[layer]), layer, hg_width, na_width)
        x2 = _out_projection(y_hg.reshape(bsz * seq, hg_width), y_na.reshape(bsz * seq, na_width),
                             x2, mod4, w_out_bf, ln_g, ln_b, layer, seq, tm_out, alpha)
    return x2.reshape(bsz, seq, d)
```

```python
import functools

import numpy as np
import jax
import jax.numpy as jnp
from jax import lax
from jax.experimental import pallas as pl
from jax.experimental.pallas import tpu as pltpu

GRID_W = 64
HG_HEAD_DIM = 128
HG_CHUNK = 32
NA_HEAD_DIM = 64
NA_ROWS_MAX = 8
NA_KC = 16
LN_EPS = 1e-5
RMS_EPS = 1e-6
MASK_VALUE = -1e30
OUT_ROW_SLICES = 4
IN_ROW_SLICES = 8

LANES = 128
MXU_DIM = 256
VMEM_LIMIT_BYTES = 56 * 1024 * 1024

F32 = jnp.float32
BF16 = jnp.bfloat16
NT_DIMS = (((1,), (1,)), ((), ()))


def _params(n_axes):
    return pltpu.CompilerParams(dimension_semantics=("arbitrary",) * n_axes,
                                vmem_limit_bytes=VMEM_LIMIT_BYTES)


def _ada_kernel(c_ref, w_ref, b_ref, o_ref):
    c = c_ref[...]
    a = c * jax.nn.sigmoid(c)
    o_ref[0] = jnp.dot(a, w_ref[0], precision=lax.Precision.HIGHEST,
                       preferred_element_type=F32) + b_ref[0]


def _ada_modulation(c, ada_w, ada_b):
    depth, d, cols = ada_w.shape
    bsz = c.shape[0]
    nj = cols // d
    return pl.pallas_call(
        _ada_kernel,
        grid=(depth, nj),
        in_specs=[pl.BlockSpec((bsz, d), lambda l, j: (0, 0)),
                  pl.BlockSpec((1, d, d), lambda l, j: (l, 0, j)),
                  pl.BlockSpec((1, 1, d), lambda l, j: (l, 0, j))],
        out_specs=pl.BlockSpec((1, bsz, d), lambda l, j: (l, 0, j)),
        out_shape=jax.ShapeDtypeStruct((depth, bsz, cols), F32),
        compiler_params=_params(2),
        name="ada_modulation",
    )(c, ada_w, ada_b.reshape(depth, 1, cols))


def _inproj_kernel(x_ref, mod_ref, w_ref, o_ref, h_ref, *, d):
    first = pl.program_id(1) == 0

    @pl.when(first)
    def _():
        w = w_ref[0].astype(BF16)
        m = mod_ref[0, 0]
        sub = x_ref.shape[0] // IN_ROW_SLICES
        for c in range(IN_ROW_SLICES):
            rows = slice(c * sub, (c + 1) * sub)
            x = x_ref[rows, :]
            mu = jnp.mean(x, axis=-1, keepdims=True)
            xc = x - mu
            var = jnp.mean(xc * xc, axis=-1, keepdims=True)
            h = (xc * lax.rsqrt(var + LN_EPS) * (1.0 + m[:, d:2 * d]) + m[:, 0:d]).astype(BF16)
            h_ref[rows, :] = h
            res = jnp.dot(h, w, preferred_element_type=F32).astype(o_ref.dtype)
            for c in range(o_ref.shape[0]):
                o_ref[c, rows, :] = res[:, c * LANES:(c + 1) * LANES]

    @pl.when(jnp.logical_not(first))
    def _():
        res = jnp.dot(h_ref[...], w_ref[0].astype(BF16), preferred_element_type=F32).astype(o_ref.dtype)
        for c in range(o_ref.shape[0]):
            o_ref[c] = res[:, c * LANES:(c + 1) * LANES]


def _in_projection(x2, mod4, w_in, layer, seq, tm, tn):
    m, d = x2.shape
    n = w_in.shape[2]
    return pl.pallas_call(
        functools.partial(_inproj_kernel, d=d),
        grid=(m // tm, n // tn),
        in_specs=[pl.BlockSpec((tm, d), lambda i, j: (i, 0)),
                  pl.BlockSpec((1, 1, 1, 3 * d), lambda i, j: (layer, (i * tm) // seq, 0, 0)),
                  pl.BlockSpec((1, d, tn), lambda i, j: (layer, 0, j))],
        out_specs=pl.BlockSpec((tn // LANES, tm, LANES), lambda i, j: (j, i, 0)),
        out_shape=jax.ShapeDtypeStruct((n // LANES, m, LANES), BF16),
        scratch_shapes=[pltpu.VMEM((tm, d), BF16)],
        compiler_params=_params(2),
        name="ln_in_projection",
    )(x2, mod4, w_in)


def _chunk_sum_matrices(tile, chunk):
    t = lax.broadcasted_iota(jnp.int32, (tile, tile), 0)
    u = lax.broadcasted_iota(jnp.int32, (tile, tile), 1)
    same = (t // chunk) == (u // chunk)
    n = tile // chunk
    cj = lax.broadcasted_iota(jnp.int32, (2 * n, tile), 0)
    cu = lax.broadcasted_iota(jnp.int32, (2 * n, tile), 1)
    is_total = cj < n
    in_chunk = (cu // chunk) == jnp.where(is_total, cj, cj - n)
    pos = cu % chunk

    def as_bf16(m):
        return jnp.where(m, 1.0, 0.0).astype(BF16)

    fw = (as_bf16(same & (u <= t)), as_bf16(in_chunk & (is_total | (pos <= chunk // 2 - 1))))
    bw = (as_bf16(same & (u >= t)), as_bf16(in_chunk & (is_total | (pos >= chunk // 2))))
    return fw, bw


def _rows_to_chunks(x, chunk):
    return jnp.concatenate([jnp.broadcast_to(x[j:j + 1], (chunk, x.shape[1])) for j in range(x.shape[0])],
                           axis=0)


def _hgrn_kernel(q_ref, i_ref, zf_ref, zb_ref, g_ref, lbl_ref, ng_ref, o_ref,
                 qa_s, ka_s, qi_s, u_s, st_s, dec_s, *, layer):
    seq = q_ref.shape[1]
    dk = q_ref.shape[2]
    tile = MXU_DIM
    chunk = HG_CHUNK
    cpt = tile // chunk
    n_tiles = seq // tile
    n_chunks = seq // chunk

    lg = lbl_ref[...]
    e = jnp.exp(lg - jnp.max(lg, axis=1, keepdims=True))
    p = e / jnp.sum(e, axis=1, keepdims=True)
    if layer == 0:
        lbs = [jnp.zeros((1, dk), F32)] * 2
    else:
        lbs = [jnp.sum(p[d, 1:layer + 1, :], axis=0, keepdims=True) for d in range(2)]

    mats = _chunk_sum_matrices(tile, chunk)
    z_refs = (zf_ref, zb_ref)

    col_chunk = lax.broadcasted_iota(jnp.int32, (dk, tile), 1) // chunk

    def prep(t):
        rows = slice(t * tile, (t + 1) * tile)
        q = q_ref[0, rows, :].astype(F32)
        vt = i_ref[0, rows, :].T
        ke = []
        for d in range(2):
            z = z_refs[d][0, rows, :].astype(F32)
            lb = lbs[d]
            f = lb + (1.0 - lb) * jax.nn.sigmoid(z)
            k = 1.0 - f
            lf = jnp.log(f)
            hi = lf.astype(BF16)
            lo = (lf - hi.astype(F32)).astype(BF16)
            x2 = jnp.concatenate([hi, lo], axis=1)
            incl, per_chunk = mats[d]
            r = jnp.dot(incl, x2, preferred_element_type=F32)
            bb = r[:, :dk] + r[:, dk:]
            r = jnp.dot(per_chunk, x2, preferred_element_type=F32)
            r = r[:, :dk] + r[:, dk:]
            tot = r[0:cpt]
            mid = r[cpt:2 * cpt]
            a1 = bb - _rows_to_chunks(mid, chunk)
            qa = q * jnp.exp(a1)
            ka = k * jnp.exp(-a1)
            lanes = slice(d * dk, (d + 1) * dk)
            qa_s[d, rows, :] = qa.astype(BF16)
            ka_s[d, rows, :] = ka.astype(BF16)
            qi_s[rows, lanes] = (qa * _rows_to_chunks(jnp.exp(mid), chunk)).astype(BF16)
            ke.append((ka * _rows_to_chunks(jnp.exp(tot - mid), chunk)).astype(BF16))
            dec_s[d, t * cpt:(t + 1) * cpt, :] = jnp.exp(tot)
        zero = jnp.zeros_like(vt)
        lhs = jnp.concatenate([jnp.where(col_chunk == j, vt, zero) for j in range(cpt)], axis=0)
        u = jnp.dot(lhs, jnp.concatenate(ke, axis=1), preferred_element_type=F32)
        for j in range(cpt):
            n = t * cpt + j
            u_s[:, n * 2 * dk:(n + 1) * 2 * dk] = u[j * dk:(j + 1) * dk]

    def advance(d, n, s):
        st_s[n, d * dk:(d + 1) * dk, :] = s.astype(BF16).T
        return dec_s[d, n:n + 1, :] * s + u_s[:, n * 2 * dk + d * dk:n * 2 * dk + (d + 1) * dk]

    s = jnp.zeros((dk, dk), F32)
    for t in range(n_tiles):
        prep(t)
        for j in range(cpt):
            s = advance(0, t * cpt + j, s)
    s = jnp.zeros((dk, dk), F32)
    for n in reversed(range(n_chunks)):
        s = advance(1, n, s)

    tt = lax.broadcasted_iota(jnp.int32, (tile, tile), 0)
    uu = lax.broadcasted_iota(jnp.int32, (tile, tile), 1)
    same = (tt // chunk) == (uu // chunk)
    causal = (same & (uu <= tt), same & (uu >= tt))
    ng = ng_ref[0]

    for t in reversed(range(n_tiles)):
        rows = slice(t * tile, (t + 1) * tile)
        v = i_ref[0, rows, :]
        pm = []
        for d in range(2):
            sc = lax.dot_general(qa_s[d, rows, :], ka_s[d, rows, :], NT_DIMS, preferred_element_type=F32)
            pm.append(jnp.where(causal[d], sc, 0.0).astype(BF16))
        o = jnp.dot(jnp.concatenate(pm, axis=1), jnp.concatenate([v, v], axis=0), preferred_element_type=F32)
        qi = qi_s[rows, :]
        o = o + jnp.concatenate([jnp.dot(qi[j * chunk:(j + 1) * chunk], st_s[t * cpt + j],
                                         preferred_element_type=F32) for j in range(cpt)], axis=0)
        on = o * lax.rsqrt(jnp.mean(o * o, axis=-1, keepdims=True) + RMS_EPS) * ng
        g = g_ref[0, rows, :].astype(F32)
        o_ref[0, rows, :] = (on * (g * jax.nn.sigmoid(g))).astype(o_ref.dtype)


def _na_bias_table(rpb_l):
    n_off = 2 * NA_KC - 1
    heads, n_rel = rpb_l.shape[0], rpb_l.shape[1]
    qc = np.arange(GRID_W)
    cs = np.clip(qc - NA_KC // 2, 0, GRID_W - NA_KC)
    kc = np.arange(GRID_W)
    valid = (kc[:, None] >= cs[None, :]) & (kc[:, None] < cs[None, :] + NA_KC)
    col_off = kc[:, None] - qc[None, :] + NA_KC - 1
    onehot = (col_off[None] == np.arange(n_off)[:, None, None]) & valid[None]
    sel = np.zeros((2, n_off, GRID_W, 2, GRID_W), np.float32)
    for h in range(2):
        sel[h, :, :, h, :] = onehot
    sel = sel.reshape(2 * n_off, GRID_W, 2 * GRID_W)
    valid2 = np.concatenate([valid, valid], axis=1)
    rpb_p = rpb_l.astype(F32).reshape(heads // 2, 2, n_rel, n_off).transpose(0, 2, 1, 3)
    rpb_p = rpb_p.reshape(heads // 2, n_rel, 2 * n_off)
    tz = jnp.einsum('prc,ckn->prkn', rpb_p, jnp.asarray(sel), precision=lax.Precision.HIGHEST)
    tz = jnp.where(jnp.asarray(valid2)[None, None], tz, MASK_VALUE)
    return tz.reshape(heads // 2, n_rel * GRID_W, 2 * GRID_W)


def _na_kernel(q_ref, k_ref, v_ref, g_ref, bias_ref, o_ref, vt_s, *, rows):
    w = GRID_W
    seq = rows * w
    kr_n = min(NA_ROWS_MAX, rows)
    half = kr_n // 2
    win = kr_n * w
    tile = MXU_DIM
    scale = NA_HEAD_DIM ** -0.5
    head0 = lax.broadcasted_iota(jnp.int32, (w, LANES), 1) < NA_HEAD_DIM

    for t in range(seq // tile):
        src0 = t * tile
        vt_s[0, :, src0:src0 + tile] = v_ref[0, src0:src0 + tile, :].T
        src1 = min(src0 + w, seq - tile)
        vt_s[1, :, src1 - w:src1 - w + tile] = v_ref[0, src1:src1 + tile, :].T

    for r in range(rows):
        rs = min(max(r - half, 0), rows - kr_n)
        rel0 = rs - r + (NA_ROWS_MAX - 1)
        brows = pl.ds(rel0 * w, win)
        qrows = pl.ds(r * w, w)
        krows = pl.ds(rs * w, win)
        par = rs % 2
        vcols = pl.ds((rs - par) * w, win)
        q = q_ref[0, qrows, :] * scale
        zq = jnp.zeros_like(q)
        qbd = jnp.concatenate([jnp.where(head0, q, zq), jnp.where(head0, zq, q)], axis=0)
        st = lax.dot_general(k_ref[0, krows, :], qbd, NT_DIMS, preferred_element_type=F32)
        st = st + bias_ref[0, brows, :]
        m = jnp.max(st, axis=0, keepdims=True)
        pt = jnp.exp(st - m)
        l = jnp.sum(pt, axis=0, keepdims=True)
        ot = jnp.dot(vt_s[par, :, vcols], pt.astype(BF16), preferred_element_type=F32) / l
        o2 = ot.T
        o = jnp.where(head0, o2[0:w], o2[w:2 * w])
        g = g_ref[0, qrows, :].astype(F32)
        o_ref[0, qrows, :] = (o * (g * jax.nn.sigmoid(g))).astype(o_ref.dtype)


def _mixer_kernel(hq_ref, hi_ref, hzf_ref, hzb_ref, hg_ref, lbl_ref, ng_ref,
                  nq_ref, nk_ref, nv_ref, ngate_ref, bias_ref, o_hg_ref, o_na_ref,
                  qa_s, ka_s, qi_s, u_s, st_s, dec_s, vt_s, *, layer, rows):
    _hgrn_kernel(hq_ref, hi_ref, hzf_ref, hzb_ref, hg_ref, lbl_ref, ng_ref, o_hg_ref,
                 qa_s, ka_s, qi_s, u_s, st_s, dec_s, layer=layer)
    _na_kernel(nq_ref, nk_ref, nv_ref, ngate_ref, bias_ref, o_na_ref, vt_s, rows=rows)


def _mixers(proj, lb_logits, norm_g, bias, layer, hg_width, na_width):
    _, bsz, seq, _ = proj.shape
    dk = HG_HEAD_DIM
    heads = hg_width // dk
    pairs = na_width // LANES
    assert heads == pairs and dk == LANES
    depth = lb_logits.shape[1]
    n_chunks = seq // HG_CHUNK
    na_col0 = 5 * heads

    def hg_col(group):
        return pl.BlockSpec((None, 1, seq, dk), lambda b, h: (group * heads + h, b, 0, 0))

    def na_col(group):
        return pl.BlockSpec((None, 1, seq, LANES), lambda b, h: (na_col0 + group * pairs + h, b, 0, 0))

    out_spec = pl.BlockSpec((1, seq, dk), lambda b, h: (b, 0, h))
    return pl.pallas_call(
        functools.partial(_mixer_kernel, layer=layer, rows=seq // GRID_W),
        grid=(bsz, heads),
        in_specs=[hg_col(0), hg_col(1), hg_col(2), hg_col(3), hg_col(4),
                  pl.BlockSpec((2, depth, dk), lambda b, h: (0, 0, h)),
                  pl.BlockSpec((1, 1, dk), lambda b, h: (layer, 0, 0)),
                  na_col(0), na_col(1), na_col(2), na_col(3),
                  pl.BlockSpec((1,) + bias.shape[1:], lambda b, h: (h, 0, 0))],
        out_specs=[out_spec, out_spec],
        out_shape=[jax.ShapeDtypeStruct((bsz, seq, hg_width), BF16),
                   jax.ShapeDtypeStruct((bsz, seq, na_width), BF16)],
        scratch_shapes=[pltpu.VMEM((2, seq, dk), BF16),
                        pltpu.VMEM((2, seq, dk), BF16),
                        pltpu.VMEM((seq, 2 * dk), BF16),
                        pltpu.VMEM((dk, n_chunks * 2 * dk), F32),
                        pltpu.VMEM((n_chunks, 2 * dk, dk), BF16),
                        pltpu.VMEM((2, n_chunks, dk), F32),
                        pltpu.VMEM((2, LANES, seq), BF16)],
        compiler_params=_params(2),
        name="hgrn2_and_neighbourhood_attention",
    )(proj, proj, proj, proj, proj, lb_logits, norm_g.reshape(norm_g.shape[0], 1, dk),
      proj, proj, proj, proj, bias)


def _outproj_kernel(yh_ref, yn_ref, x_ref, mod_ref, w_ref, lg_ref, lb_ref, o_ref, *, d, alpha):
    hw, nw = yh_ref.shape[1], yn_ref.shape[1]
    gate = mod_ref[0, 0][:, 2 * d:3 * d]
    sub = yh_ref.shape[0] // OUT_ROW_SLICES
    for c in range(OUT_ROW_SLICES):
        rows = slice(c * sub, (c + 1) * sub)
        y = jnp.dot(yh_ref[rows, :], w_ref[0, 0:hw, :], preferred_element_type=F32)
        y = y + jnp.dot(yn_ref[rows, :], w_ref[0, hw:hw + nw, :], preferred_element_type=F32)
        z = alpha * x_ref[rows, :] + gate * y
        mu = jnp.mean(z, axis=-1, keepdims=True)
        zc = z - mu
        var = jnp.mean(zc * zc, axis=-1, keepdims=True)
        o_ref[rows, :] = zc * lax.rsqrt(var + LN_EPS) * lg_ref[0] + lb_ref[0]


def _out_projection(y_hg2, y_na2, x2, mod4, w_out_bf, ln_g, ln_b, layer, seq, tm, alpha):
    m, d = x2.shape
    hw, nw = y_hg2.shape[1], y_na2.shape[1]
    return pl.pallas_call(
        functools.partial(_outproj_kernel, d=d, alpha=alpha),
        grid=(m // tm,),
        in_specs=[pl.BlockSpec((tm, hw), lambda i: (i, 0)),
                  pl.BlockSpec((tm, nw), lambda i: (i, 0)),
                  pl.BlockSpec((tm, d), lambda i: (i, 0)),
                  pl.BlockSpec((1, 1, 1, 3 * d), lambda i: (layer, (i * tm) // seq, 0, 0)),
                  pl.BlockSpec((1, hw + nw, d), lambda i: (layer, 0, 0)),
                  pl.BlockSpec((1, 1, d), lambda i: (layer, 0, 0)),
                  pl.BlockSpec((1, 1, d), lambda i: (layer, 0, 0))],
        out_specs=pl.BlockSpec((tm, d), lambda i: (i, 0)),
        out_shape=jax.ShapeDtypeStruct((m, d), F32),
        compiler_params=_params(1),
        name="out_projection_deepnorm_ln",
    )(y_hg2, y_na2, x2, mod4, w_out_bf, ln_g.reshape(-1, 1, d), ln_b.reshape(-1, 1, d))


def _projection_tiles(m, seq, n_in):
    tm_in = seq
    tn_in = max(t for t in range(MXU_DIM, 4 * MXU_DIM + 1, MXU_DIM) if n_in % t == 0)
    tm_out = seq // 2
    assert m % tm_in == 0 and m % tm_out == 0 and seq % tm_out == 0
    return tm_in, tn_in, tm_out


def kernel(x, c, ada_w, ada_b, w_in, lb_logits, hg_norm_g, rpb, w_out, ln_g, ln_b):
    bsz, seq, d = x.shape
    depth = w_in.shape[0]
    mix = w_out.shape[1]
    hg_width = mix // 2
    na_width = mix - hg_width
    alpha = (2 * depth) ** 0.25
    tm_in, tn_in, tm_out = _projection_tiles(bsz * seq, seq, w_in.shape[2])

    mod = _ada_modulation(c, ada_w, ada_b)
    mod4 = mod.reshape(depth, bsz, 1, 3 * d)
    w_out_bf = w_out.astype(BF16)

    x2 = x.reshape(bsz * seq, d)
    for layer in range(depth):
        proj = _in_projection(x2, mod4, w_in, layer, seq, tm_in, tn_in).reshape(-1, bsz, seq, LANES)
        y_hg, y_na = _mixers(proj, lb_logits, hg_norm_g, _na_bias_table(rpb[layer]), layer, hg_width, na_width)
        x2 = _out_projection(y_hg.reshape(bsz * seq, hg_width), y_na.reshape(bsz * seq, na_width),
                             x2, mod4, w_out_bf, ln_g, ln_b, layer, seq, tm_out, alpha)
    return x2.reshape(bsz, seq, d)
```
